```python
import jax, jax.numpy as jnp
from jax import lax
import numpy as np

D_MODEL = 1024
BATCH = 4
SEQ = 4096
DEPTH = 4

HEAD_DIM = 64
A_HEADS = 8
A_KV_HEADS = 2
A_WIDTH = A_HEADS * HEAD_DIM
A_KV_WIDTH = A_KV_HEADS * HEAD_DIM
IDX_HEADS = 4
IDX_DIM = 64
IDX_SCALE = (IDX_HEADS * IDX_DIM) ** -0.5
TOPK_MAX = 256
POOL_WINDOWS = (2, 4, 8, 16)
POOL_GROUP = 128
POOL_WIDTH = POOL_GROUP * len(POOL_WINDOWS)
EVEN_MIX_WIDTH = A_WIDTH + POOL_WIDTH
EVEN_IN_COLS = A_WIDTH + 2 * A_KV_WIDTH + IDX_HEADS * IDX_DIM + IDX_DIM + IDX_HEADS + POOL_WIDTH
C_HEADS = 16
C_WIDTH = C_HEADS * HEAD_DIM
D_FF = -(-8 * D_MODEL // (3 * 256)) * 256
N_EVEN = (DEPTH + 1) // 2
N_ODD = DEPTH // 2
Q_BLOCK = 128
EPS = 1e-6

kernel_name = "hybrid_dsa_pool_stickbreak_trunk"


def rms_norm(x, g):
    x32 = x.astype(jnp.float32)
    y = x32 * lax.rsqrt(jnp.mean(x32 * x32, axis=-1, keepdims=True) + EPS)
    return (y * g.astype(jnp.float32)).astype(x.dtype)


def to_blocks(a):
    b, t = a.shape[:2]
    a = a.reshape((b, t // Q_BLOCK, Q_BLOCK) + a.shape[2:])
    return jnp.moveaxis(a, 1, 0)


def from_blocks(a):
    a = jnp.moveaxis(a, 0, 1)
    return a.reshape((a.shape[0], a.shape[1] * a.shape[2]) + a.shape[3:])


def dsa_attention(q, k, v, q_idx, k_idx, w_idx):
    b, t = q.shape[:2]
    n_sel = min(TOPK_MAX, t // 4)
    rep = A_HEADS // A_KV_HEADS
    key_pos = jnp.arange(t)
    gather = jax.vmap(lambda kv, ids: kv[ids])

    def block(args):
        qb, qib, wb, start = args
        q_pos = start + jnp.arange(Q_BLOCK)
        causal = key_pos[None, :] <= q_pos[:, None]
        rel = jax.nn.relu(jnp.einsum('bthd,bsd->bths', qib, k_idx).astype(jnp.float32))
        score = jnp.einsum('bths,bth->bts', rel, wb.astype(jnp.float32)) * IDX_SCALE
        score = jnp.where(causal[None], score, -jnp.inf)
        _, sel = lax.top_k(score, n_sel)
        k_sel = gather(k, sel)
        v_sel = gather(v, sel)
        valid = sel <= q_pos[None, :, None]
        qg = qb.reshape(b, Q_BLOCK, A_KV_HEADS, rep, HEAD_DIM)
        logits = jnp.einsum('btgrd,btkgd->btgrk', qg, k_sel).astype(jnp.float32) * (HEAD_DIM ** -0.5)
        logits = jnp.where(valid[:, :, None, None, :], logits, -jnp.inf)
        p = jax.nn.softmax(logits, axis=-1).astype(v.dtype)
        o = jnp.einsum('btgrk,btkgd->btgrd', p, v_sel)
        return o.reshape(b, Q_BLOCK, A_WIDTH)

    starts = jnp.arange(t // Q_BLOCK) * Q_BLOCK
    out = lax.map(block, (to_blocks(q), to_blocks(q_idx), to_blocks(w_idx), starts))
    return from_blocks(out)


def multiscale_pool(u, pool_w, pool_scale):
    b, t, _ = u.shape
    groups = u.astype(jnp.float32).reshape(b, t, len(POOL_WINDOWS), POOL_GROUP)
    c0 = jnp.pad(jnp.cumsum(groups, axis=1), ((0, 0), (1, 0), (0, 0), (0, 0)))
    pos = jnp.arange(1, t + 1, dtype=jnp.float32)
    outs = []
    for g, w in enumerate(POOL_WINDOWS):
        cg = c0[:, :, g]
        hi = cg[:, 1:]
        lo = jnp.pad(cg[:, :t + 1 - w], ((0, 0), (w - 1, 0), (0, 0)))
        count = jnp.minimum(pos, float(w))[None, :, None]
        outs.append((hi - lo) / count - groups[:, :, g])
    mixed = jnp.stack(outs, axis=2)
    y = jnp.einsum('btgc,gcd->btgd', mixed, pool_w.astype(jnp.float32))
    return (y.reshape(b, t, POOL_WIDTH) * pool_scale.astype(jnp.float32)).astype(u.dtype)


def stick_breaking_attention(q, k, v):
    b, t = q.shape[:2]
    key_pos = jnp.arange(t)

    def block(args):
        qb, start = args
        q_pos = start + jnp.arange(Q_BLOCK)
        strict = key_pos[None, :] < q_pos[:, None]
        z = jnp.einsum('bthd,bshd->bhts', qb, k).astype(jnp.float32) * (HEAD_DIM ** -0.5)
        brk = jnp.where(strict, jax.nn.softplus(z), 0.0)
        later = jnp.pad(brk[..., 1:], ((0, 0), (0, 0), (0, 0), (0, 1)))
        remain = lax.cumsum(later, axis=3, reverse=True)
        log_a = jax.nn.log_sigmoid(z) - remain
        a = jnp.where(strict, jnp.exp(log_a), 0.0).astype(v.dtype)
        o = jnp.einsum('bhts,bshd->bthd', a, v)
        return o.reshape(b, Q_BLOCK, C_WIDTH)

    starts = jnp.arange(t // Q_BLOCK) * Q_BLOCK
    out = lax.map(block, (to_blocks(q), starts))
    return from_blocks(out)


def even_mixer(h, w_in, w_out, q_g, k_g, pool_w, pool_scale):
    b, t, _ = h.shape
    proj = h @ w_in
    sizes = [A_WIDTH, A_KV_WIDTH, A_KV_WIDTH, IDX_HEADS * IDX_DIM, IDX_DIM, IDX_HEADS]
    offsets = np.cumsum(sizes).tolist()
    q, k, v, qi, ki, wi, u = jnp.split(proj, offsets, axis=-1)
    q = rms_norm(q.reshape(b, t, A_HEADS, HEAD_DIM), q_g)
    k = rms_norm(k.reshape(b, t, A_KV_HEADS, HEAD_DIM), k_g)
    v = v.reshape(b, t, A_KV_HEADS, HEAD_DIM)
    attn = dsa_attention(q, k, v, qi.reshape(b, t, IDX_HEADS, IDX_DIM), ki, wi)
    pooled = multiscale_pool(u, pool_w, pool_scale)
    return jnp.concatenate([attn, pooled], axis=-1) @ w_out


def odd_mixer(h, w_qkv, w_out):
    b, t, _ = h.shape
    q, k, v = jnp.split((h @ w_qkv).reshape(b, t, 3, C_HEADS, HEAD_DIM), 3, axis=2)
    o = stick_breaking_attention(q[:, :, 0], k[:, :, 0], v[:, :, 0])
    return o @ w_out


def swiglu(h, w1, w3, w2):
    return (jax.nn.silu(h @ w1) * (h @ w3)) @ w2


def setup_inputs(seed: int = 0) -> dict:
    key = jax.random.key(seed)
    ks = jax.random.split(key, 16)
    f32 = jnp.float32

    def dense(k, shape):
        return jax.random.normal(k, shape, f32) * (shape[-2] ** -0.5)

    def gain(k, shape, s):
        return 1.0 + s * jax.random.normal(k, shape, f32)

    return {
        'x': jax.random.normal(ks[0], (BATCH, SEQ, D_MODEL), f32),
        'norm_mix_g': gain(ks[1], (DEPTH, D_MODEL), 0.05),
        'norm_ffn_g': gain(ks[2], (DEPTH, D_MODEL), 0.05),
        'ev_w_in': dense(ks[3], (N_EVEN, D_MODEL, EVEN_IN_COLS)),
        'ev_w_out': dense(ks[4], (N_EVEN, EVEN_MIX_WIDTH, D_MODEL)),
        'ev_q_norm_g': gain(ks[5], (N_EVEN, HEAD_DIM), 0.05),
        'ev_k_norm_g': gain(ks[6], (N_EVEN, HEAD_DIM), 0.05),
        'ev_pool_w': dense(ks[7], (N_EVEN, len(POOL_WINDOWS), POOL_GROUP, POOL_GROUP)),
        'ev_pool_scale': gain(ks[8], (N_EVEN, POOL_WIDTH), 0.1),
        'od_w_qkv': dense(ks[9], (N_ODD, D_MODEL, 3 * C_WIDTH)),
        'od_w_out': dense(ks[10], (N_ODD, C_WIDTH, D_MODEL)),
        'ffn_w1': dense(ks[11], (DEPTH, D_MODEL, D_FF)),
        'ffn_w3': dense(ks[12], (DEPTH, D_MODEL, D_FF)),
        'ffn_w2': dense(ks[13], (DEPTH, D_FF, D_MODEL)),
    }


def reference(x, norm_mix_g, norm_ffn_g, ev_w_in, ev_w_out, ev_q_norm_g, ev_k_norm_g,
              ev_pool_w, ev_pool_scale, od_w_qkv, od_w_out, ffn_w1, ffn_w3, ffn_w2):
    for layer in range(DEPTH):
        i = layer // 2
        h = rms_norm(x, norm_mix_g[layer])
        if layer % 2 == 0:
            x = x + even_mixer(h, ev_w_in[i], ev_w_out[i], ev_q_norm_g[i], ev_k_norm_g[i],
                               ev_pool_w[i], ev_pool_scale[i])
        else:
            x = x + odd_mixer(h, od_w_qkv[i], od_w_out[i])
        h = rms_norm(x, norm_ffn_g[layer])
        x = x + swiglu(h, ffn_w1[layer], ffn_w3[layer], ffn_w2[layer])
    return x
```

```python
import functools

import jax
import jax.numpy as jnp
import numpy as np
from jax import lax
from jax.experimental import pallas as pl
from jax.experimental.pallas import tpu as pltpu

F32 = jnp.float32
BF16 = jnp.bfloat16

HEAD_DIM = 64
A_HEADS = 8
A_KV_HEADS = 2
IDX_HEADS = 4
IDX_DIM = 64
TOPK_MAX = 256
POOL_WINDOWS = (2, 4, 8, 16)
POOL_GROUP = 128
POOL_HALO = 16
C_HEADS = 16
EPS = 1e-6

LANES = 128
INT_MIN = np.int32(-(2**31))
MASK_BIAS = -1e30
EXP_UNDERFLOW = 105.0
VMEM_LIMIT = 56 * 1024 * 1024

_NT = (((1,), (1,)), ((), ()))


def _dot(a, b):
    return jnp.dot(a, b, preferred_element_type=F32)


def _dot_nt(a, b):
    return lax.dot_general(a, b, _NT, preferred_element_type=F32)


def _split_bf16(x):
    hi = x.astype(BF16)
    lo = (x - hi.astype(F32)).astype(BF16)
    return hi, lo


def _rms_rows(x, g):
    ms = jnp.mean(x * x, axis=-1, keepdims=True)
    return x * lax.rsqrt(ms + EPS) * g


def _const_spec(shape):
    nd = len(shape)
    return pl.BlockSpec(shape, lambda *_: (0,) * nd, pipeline_mode=pl.Buffered(1))


def _params(sem):
    return pltpu.CompilerParams(dimension_semantics=sem, vmem_limit_bytes=VMEM_LIMIT)


QP_W = A_HEADS * LANES
QIP_W = IDX_HEADS * LANES
EVEN_COLS = QP_W + LANES + LANES + QIP_W + LANES + 4 * POOL_GROUP
_OFF_K = QP_W
_OFF_V = _OFF_K + LANES
_OFF_QI = _OFF_V + LANES
_OFF_KIW = _OFF_QI + QIP_W
_OFF_U = _OFF_KIW + LANES


def _even_proj_kernel(x_ref, g_ref, w_ref, qg_ref, kg_ref, ones_ref, seg_ref,
                      qp_ref, k_ref, vaug_ref, qip_ref, kiwb_ref, kiwf_ref, u_ref):
    h = _rms_rows(x_ref[...], g_ref[...]).astype(BF16)
    tm = h.shape[0]

    def head_norm(y, seg, gain):
        hi, lo = _split_bf16(y * y)
        ss = _dot(hi, seg) + _dot(lo, seg)
        return y * lax.rsqrt(ss * (1.0 / HEAD_DIM) + EPS) * gain

    ones = ones_ref[...]
    for hh in range(A_HEADS):
        sl = slice(hh * LANES, (hh + 1) * LANES)
        y = _dot(h, w_ref[:, sl])
        qp_ref[:, sl] = head_norm(y, ones, qg_ref[:, sl]).astype(BF16)

    y = _dot(h, w_ref[:, _OFF_K:_OFF_K + LANES])
    k_ref[...] = head_norm(y, seg_ref[...], kg_ref[...]).astype(BF16)

    v = _dot(h, w_ref[:, _OFF_V:_OFF_V + LANES])
    lo_half = lax.broadcasted_iota(jnp.int32, (tm, LANES), 1) < HEAD_DIM
    vaug_ref[:, 0:LANES] = jnp.where(lo_half, v, 1.0).astype(BF16)
    vaug_ref[:, LANES:2 * LANES] = jnp.where(lo_half, 1.0, v).astype(BF16)

    qip_ref[...] = _dot(h, w_ref[:, _OFF_QI:_OFF_QI + QIP_W]).astype(BF16)
    kiw = _dot(h, w_ref[:, _OFF_KIW:_OFF_KIW + LANES])
    kiwb_ref[...] = kiw.astype(BF16)
    kiwf_ref[...] = kiw
    u_ref[...] = _dot(h, w_ref[:, _OFF_U:_OFF_U + 4 * POOL_GROUP])


def _even_proj(x, g, w, qg, kg, tm):
    n, d = x.shape
    ones = jnp.ones((LANES, LANES), BF16)
    r = np.arange(LANES) // HEAD_DIM
    seg = jnp.asarray(r[:, None] == r[None, :], BF16)
    row = lambda wd: pl.BlockSpec((tm, wd), lambda i: (i, 0))
    outs = [(QP_W, BF16), (LANES, BF16), (2 * LANES, BF16), (QIP_W, BF16), (LANES, BF16),
            (LANES, F32), (4 * POOL_GROUP, F32)]
    return pl.pallas_call(
        _even_proj_kernel,
        grid=(n // tm,),
        in_specs=[row(d), _const_spec((1, d)), _const_spec((d, EVEN_COLS)), _const_spec((1, QP_W)),
                  _const_spec((1, LANES)), _const_spec((LANES, LANES)), _const_spec((LANES, LANES))],
        out_specs=[row(wd) for wd, _ in outs],
        out_shape=[jax.ShapeDtypeStruct((n, wd), dt) for wd, dt in outs],
        compiler_params=_params(("parallel",)),
        name="even_proj",
    )(x, g, w, qg, kg, ones, seg)


def _dsa_kernel(qp_ref, qip_ref, wq_ref, k_ref, vaug_ref, kiw_ref, o_ref,
                keys_ref, bias_ref, wb_ref, *, tq, n_sel, pos_bits):
    i = pl.program_id(1)
    nk = i + 1
    row = lax.broadcasted_iota(jnp.int32, (tq, tq), 0)
    col = lax.broadcasted_iota(jnp.int32, (tq, tq), 1)
    causal = col <= row
    halves = tq // LANES

    wq = wq_ref[...]
    for hh in range(IDX_HEADS):
        wb_ref[hh] = jnp.broadcast_to(wq[:, IDX_DIM + hh:IDX_DIM + hh + 1], (tq, LANES))

    def score_chunk(c, diagonal):
        kb = kiw_ref[pl.ds(pl.multiple_of(c * tq, tq), tq), :]
        s = None
        for hh in range(IDX_HEADS):
            d = _dot_nt(qip_ref[:, hh * LANES:(hh + 1) * LANES], kb)
            r = jnp.maximum(d, 0.0)
            wb = wb_ref[hh]
            term = jnp.concatenate([r[:, j * LANES:(j + 1) * LANES] * wb for j in range(halves)], axis=1)
            s = term if s is None else s + term
        bits = lax.bitcast_convert_type(s, jnp.int32)
        key = bits ^ ((bits >> 31) & jnp.int32(0x7FFFFFFF))
        if diagonal:
            key = jnp.where(causal, key, INT_MIN)
        keys_ref[c] = key

    def score_body(c, carry):
        score_chunk(c, False)
        return carry

    lax.fori_loop(0, i, score_body, 0)
    score_chunk(i, True)

    def count(pred):
        def body(c, acc):
            m = pred(keys_ref[c], c)
            for j in range(halves):
                acc = acc + jnp.where(m[:, j * LANES:(j + 1) * LANES], 1, 0)
            return acc
        acc = lax.fori_loop(0, nk, body, jnp.zeros((tq, LANES), jnp.int32))
        return jnp.sum(acc, axis=1, keepdims=True)

    def bisect_value(it, thr):
        cand = thr ^ lax.shift_left(jnp.int32(1), 31 - it)
        cnt = count(lambda kc, c: kc >= cand)
        return jnp.where(cnt >= n_sel, cand, thr)

    thr = lax.fori_loop(0, 32, bisect_value, jnp.full((tq, 1), INT_MIN, jnp.int32))
    thr = jnp.maximum(thr, INT_MIN + 1)
    cnt_ge = count(lambda kc, c: kc >= thr)
    overflow = jnp.max(jnp.where(cnt_ge > n_sel, 1, 0))

    @pl.when(overflow == 0)
    def _():
        def body(c, carry):
            bias_ref[c] = jnp.where(keys_ref[c] >= thr, 0.0, MASK_BIAS)
            return carry
        lax.fori_loop(0, nk, body, 0)

    @pl.when(overflow != 0)
    def _():
        need = n_sel - count(lambda kc, c: kc > thr)

        def bisect_pos(it, lim):
            cand = lim | lax.shift_left(jnp.int32(1), pos_bits - 1 - it)
            cnt = count(lambda kc, c: (kc == thr) & (col + c * tq < cand))
            return jnp.where(cnt <= need, cand, lim)

        lim = lax.fori_loop(0, pos_bits, bisect_pos, jnp.zeros((tq, 1), jnp.int32))

        def body(c, carry):
            kc = keys_ref[c]
            sel = (kc > thr) | ((kc == thr) & (col + c * tq < lim))
            bias_ref[c] = jnp.where(sel, 0.0, MASK_BIAS)
            return carry
        lax.fori_loop(0, nk, body, 0)

    lo_half = lax.broadcasted_iota(jnp.int32, (tq, LANES), 1) < HEAD_DIM
    outs = []
    for hh in range(A_HEADS):
        g = hh // (A_HEADS // A_KV_HEADS)
        qh = qp_ref[:, hh * LANES:(hh + 1) * LANES]

        def logits(c, qh=qh):
            kb = k_ref[pl.ds(pl.multiple_of(c * tq, tq), tq), :]
            return _dot_nt(qh, kb) + bias_ref[c]

        def max_body(c, m):
            l = logits(c)
            for j in range(halves):
                m = jnp.maximum(m, l[:, j * LANES:(j + 1) * LANES])
            return m

        m = lax.fori_loop(0, nk, max_body, jnp.full((tq, LANES), MASK_BIAS, F32))
        m = jnp.broadcast_to(jnp.max(m, axis=1, keepdims=True), (tq, LANES))
        m = jnp.concatenate([m] * halves, axis=1)

        def pv_body(c, acc, g=g, m=m):
            p = jnp.exp(logits(c) - m).astype(BF16)
            vb = vaug_ref[pl.ds(pl.multiple_of(c * tq, tq), tq), g * LANES:(g + 1) * LANES]
            return acc + _dot(p, vb)

        acc = lax.fori_loop(0, nk, pv_body, jnp.zeros((tq, LANES), F32))
        o = acc / pltpu.roll(acc, HEAD_DIM, axis=1)
        if (hh % 2 == 1) != (g == 1):
            o = pltpu.roll(o, HEAD_DIM, axis=1)
        outs.append(o)
    for pr in range(A_HEADS // 2):
        o_ref[:, pr * LANES:(pr + 1) * LANES] = jnp.where(lo_half, outs[2 * pr], outs[2 * pr + 1]).astype(BF16)


def _dsa_attention(qp, k, vaug, qip, kiwb, kiwf, batch, seq, tq):
    n = qp.shape[0]
    nq = seq // tq
    n_sel = min(TOPK_MAX, seq // 4)
    pos_bits = int(seq).bit_length()
    qrow = lambda wd: pl.BlockSpec((tq, wd), lambda b, i: (b * nq + i, 0))
    full = lambda wd: pl.BlockSpec((seq, wd), lambda b, i: (b, 0))
    kern = functools.partial(_dsa_kernel, tq=tq, n_sel=n_sel, pos_bits=pos_bits)
    return pl.pallas_call(
        kern,
        grid=(batch, nq),
        in_specs=[qrow(QP_W), qrow(QIP_W), qrow(LANES), full(LANES), full(2 * LANES), full(LANES)],
        out_specs=qrow(A_HEADS * HEAD_DIM),
        out_shape=jax.ShapeDtypeStruct((n, A_HEADS * HEAD_DIM), BF16),
        scratch_shapes=[pltpu.VMEM((nq, tq, tq), jnp.int32), pltpu.VMEM((nq, tq, tq), F32),
                        pltpu.VMEM((IDX_HEADS, tq, LANES), F32)],
        compiler_params=_params(("parallel", "arbitrary")),
        name="dsa_attention",
    )(qp, qip, kiwf, k, vaug, kiwb)


def _odd_proj_kernel(x_ref, g_ref, w_ref, o_ref, *, q_cols):
    h = _rms_rows(x_ref[...], g_ref[...]).astype(BF16)
    cols = w_ref.shape[1]
    step = 512
    for c0 in range(0, cols, step):
        y = _dot(h, w_ref[:, c0:c0 + step])
        if c0 < q_cols:
            y = y * (HEAD_DIM ** -0.5)
        o_ref[:, c0:c0 + step] = y.astype(BF16)


def _odd_proj(x, g, w, tm):
    n, d = x.shape
    cols = w.shape[1]
    kern = functools.partial(_odd_proj_kernel, q_cols=cols // 3)
    return pl.pallas_call(
        kern,
        grid=(n // tm,),
        in_specs=[pl.BlockSpec((tm, d), lambda i: (i, 0)), _const_spec((1, d)), _const_spec((d, cols))],
        out_specs=pl.BlockSpec((tm, cols), lambda i: (i, 0)),
        out_shape=jax.ShapeDtypeStruct((n, cols), BF16),
        compiler_params=_params(("parallel",)),
        name="odd_proj",
    )(x, g, w)


def _sb_kernel(q_ref, k_ref, v_ref, tri_ref, o_ref, *, tq):
    i = pl.program_id(2)
    row = lax.broadcasted_iota(jnp.int32, (tq, tq), 0)
    col = lax.broadcasted_iota(jnp.int32, (tq, tq), 1)
    strict = col < row
    lo_half = lax.broadcasted_iota(jnp.int32, (tq, LANES), 1) < HEAD_DIM
    q = q_ref[...]
    outs = []
    for hh in range(2):
        qh = jnp.where(lo_half if hh == 0 else jnp.logical_not(lo_half), q, jnp.zeros_like(q))

        def block(j, diagonal, carry, acc, qh=qh):
            off = pl.multiple_of(j * tq, tq)
            z = _dot_nt(qh, k_ref[pl.ds(off, tq), :])
            l1p = jnp.log(1.0 + jnp.exp(-jnp.abs(z)))
            sp = jnp.maximum(z, 0.0) + l1p
            ls = jnp.minimum(z, 0.0) - l1p
            if diagonal:
                sp = jnp.where(strict, sp, 0.0)
            hi, lo = _split_bf16(sp)
            later = _dot(hi, tri_ref[...]) + _dot(lo, tri_ref[...])
            a = jnp.exp(ls - later - carry)
            if diagonal:
                a = jnp.where(strict, a, 0.0)
            acc = acc + _dot(a.astype(BF16), v_ref[pl.ds(off, tq), :])
            carry = carry + jnp.sum(sp, axis=1, keepdims=True)
            return carry, acc

        carry, acc = block(i, True, jnp.zeros((tq, 1), F32), jnp.zeros((tq, LANES), F32))

        def cond(st):
            return jnp.logical_and(st[0] >= 0, st[1] > 0)

        def body(st, block=block):
            j, _, carry, acc = st
            carry, acc = block(j, False, carry, acc)
            go = (jnp.min(carry) <= EXP_UNDERFLOW).astype(jnp.int32)
            return j - 1, go, carry, acc

        go0 = (jnp.min(carry) <= EXP_UNDERFLOW).astype(jnp.int32)
        _, _, _, acc = lax.while_loop(cond, body, (i - 1, go0, carry, acc))
        outs.append(acc)
    o_ref[...] = jnp.where(lo_half, outs[0], outs[1]).astype(BF16)


def _sb_attention(qkv, batch, seq, tq):
    n = qkv.shape[0]
    width = qkv.shape[1] // 3
    pairs = width // LANES
    nq = seq // tq
    r = np.arange(tq)
    tri = jnp.asarray(r[:, None] > r[None, :], BF16)
    kern = functools.partial(_sb_kernel, tq=tq)
    return pl.pallas_call(
        kern,
        grid=(batch, pairs, nq),
        in_specs=[pl.BlockSpec((tq, LANES), lambda b, p, i: (b * nq + i, p)),
                  pl.BlockSpec((seq, LANES), lambda b, p, i: (b, pairs + p)),
                  pl.BlockSpec((seq, LANES), lambda b, p, i: (b, 2 * pairs + p)),
                  _const_spec((tq, tq))],
        out_specs=pl.BlockSpec((tq, LANES), lambda b, p, i: (b * nq + i, p)),
        out_shape=jax.ShapeDtypeStruct((n, width), BF16),
        compiler_params=_params(("parallel", "parallel", "arbitrary")),
        name="sb_attention",
    )(qkv, qkv, qkv, tri)


def _pooled(u_ref, halo_ref, pw_ref, ps_ref, ext_ref, seq, tm):
    tpos0 = (pl.program_id(0) * tm) % seq
    halo = jnp.where(tpos0 == 0, 0.0, halo_ref[...])
    ext_ref[0:POOL_HALO, :] = halo
    ext_ref[POOL_HALO:POOL_HALO + tm, :] = u_ref[...]
    t_idx = lax.broadcasted_iota(jnp.int32, (tm, POOL_GROUP), 0) + tpos0
    out = []
    for gi, w in enumerate(POOL_WINDOWS):
        sl = slice(gi * POOL_GROUP, (gi + 1) * POOL_GROUP)
        s = ext_ref[POOL_HALO:POOL_HALO + tm, sl]
        for dlt in range(1, w):
            s = s + ext_ref[POOL_HALO - dlt:POOL_HALO - dlt + tm, sl]
        cnt = jnp.minimum(t_idx + 1, w).astype(F32)
        mixed = s / cnt - u_ref[:, sl]
        y = _dot(mixed.astype(BF16), pw_ref[gi]) * ps_ref[:, sl]
        out.append(y.astype(BF16))
    return out


def _post_kernel(*refs, even, seq, tm):
    if even:
        (attn_ref, u_ref, halo_ref, pw_ref, ps_ref, wo_ref, x_ref, g_ref, w1_ref, w3_ref, w2_ref,
         o_ref, acc_ref, ext_ref) = refs
        half = attn_ref.shape[1]
        y = _dot(attn_ref[...], wo_ref[0:half, :])
        for gi, pg in enumerate(_pooled(u_ref, halo_ref, pw_ref, ps_ref, ext_ref, seq, tm)):
            y = y + _dot(pg, wo_ref[half + gi * POOL_GROUP:half + (gi + 1) * POOL_GROUP, :])
    else:
        attn_ref, wo_ref, x_ref, g_ref, w1_ref, w3_ref, w2_ref, o_ref, acc_ref = refs
        y = _dot(attn_ref[...], wo_ref[...])
    x1 = x_ref[...] + y
    h = _rms_rows(x1, g_ref[...]).astype(BF16)
    acc_ref[...] = x1

    def ffn_chunk(c, carry):
        a = _dot(h, w1_ref[c])
        b = _dot(h, w3_ref[c])
        gate = (a * (1.0 / (1.0 + jnp.exp(-a))) * b).astype(BF16)
        acc_ref[...] += _dot(gate, w2_ref[c])
        return carry

    lax.fori_loop(0, w1_ref.shape[0], ffn_chunk, 0)
    o_ref[...] = acc_ref[...]


def _post(x, attn, w_out, g, w1, w3, w2, tm, seq, pool=None):
    n, d = x.shape
    even = pool is not None
    row = lambda wd: pl.BlockSpec((tm, wd), lambda i: (i, 0))
    ffn_specs = [_const_spec(w_out.shape), row(d), _const_spec((1, d)),
                 _const_spec(w1.shape), _const_spec(w3.shape), _const_spec(w2.shape)]
    scratch = [pltpu.VMEM((tm, d), F32)]
    if even:
        u, pool_w, pool_scale = pool
        halo_blocks = tm // POOL_HALO
        halo = pl.BlockSpec((POOL_HALO, u.shape[1]), lambda i: (jnp.maximum(i * halo_blocks - 1, 0), 0))
        in_specs = [row(attn.shape[1]), row(u.shape[1]), halo, _const_spec(pool_w.shape),
                    _const_spec(pool_scale.shape)] + ffn_specs
        args = (attn, u, u, pool_w, pool_scale, w_out, x, g, w1, w3, w2)
        scratch.append(pltpu.VMEM((POOL_HALO + tm, u.shape[1]), F32))
    else:
        in_specs = [row(attn.shape[1])] + ffn_specs
        args = (attn, w_out, x, g, w1, w3, w2)
    kern = functools.partial(_post_kernel, even=even, seq=seq, tm=tm)
    return pl.pallas_call(
        kern,
        grid=(n // tm,),
        in_specs=in_specs,
        out_specs=row(d),
        out_shape=jax.ShapeDtypeStruct((n, d), F32),
        scratch_shapes=scratch,
        compiler_params=_params(("parallel",)),
        name="post_even" if even else "post_odd",
    )(*args)


def _even_in_weights(w_in, q_g, k_g):
    d = w_in.shape[0]
    a_w, kv_w, qi_w = A_HEADS * HEAD_DIM, A_KV_HEADS * HEAD_DIM, IDX_HEADS * IDX_DIM
    o_k, o_v, o_qi = a_w, a_w + kv_w, a_w + 2 * kv_w
    o_ki = o_qi + qi_w
    o_wi = o_ki + IDX_DIM
    o_u = o_wi + IDX_HEADS
    zeros = lambda wd: jnp.zeros((d, wd), w_in.dtype)
    cols, qg = [], []
    rep = A_HEADS // A_KV_HEADS
    for hh in range(A_HEADS):
        wq = w_in[:, hh * HEAD_DIM:(hh + 1) * HEAD_DIM]
        gq = q_g * (HEAD_DIM ** -0.5)
        if hh // rep == 0:
            cols += [wq, zeros(HEAD_DIM)]
            qg += [gq, jnp.zeros_like(gq)]
        else:
            cols += [zeros(HEAD_DIM), wq]
            qg += [jnp.zeros_like(gq), gq]
    cols += [w_in[:, o_k:o_v], w_in[:, o_v:o_qi]]
    for hh in range(IDX_HEADS):
        cols += [w_in[:, o_qi + hh * IDX_DIM:o_qi + (hh + 1) * IDX_DIM], zeros(LANES - IDX_DIM)]
    idx_scale = (IDX_HEADS * IDX_DIM) ** -0.5
    cols += [w_in[:, o_ki:o_wi], w_in[:, o_wi:o_u] * idx_scale, zeros(LANES - IDX_DIM - IDX_HEADS)]
    cols += [w_in[:, o_u:]]
    w = jnp.concatenate(cols, axis=1).astype(BF16)
    return w, jnp.concatenate(qg)[None, :], jnp.tile(k_g, A_KV_HEADS)[None, :]


def _ffn_weights(w1, w3, w2, fc):
    d, f = w1.shape
    nc = f // fc
    w1c = w1.reshape(d, nc, fc).transpose(1, 0, 2).astype(BF16)
    w3c = w3.reshape(d, nc, fc).transpose(1, 0, 2).astype(BF16)
    w2c = w2.reshape(nc, fc, d).astype(BF16)
    return w1c, w3c, w2c


def _trunk(x, norm_mix_g, norm_ffn_g, ev_w_in, ev_w_out, ev_q_norm_g, ev_k_norm_g, ev_pool_w,
           ev_pool_scale, od_w_qkv, od_w_out, ffn_w1, ffn_w3, ffn_w2, *, tm, tq, fc):
    batch, seq, d = x.shape
    depth = norm_mix_g.shape[0]
    xf = x.reshape(batch * seq, d)
    for layer in range(depth):
        li = layer // 2
        g_mix = norm_mix_g[layer][None, :]
        g_ffn = norm_ffn_g[layer][None, :]
        w1, w3, w2 = _ffn_weights(ffn_w1[layer], ffn_w3[layer], ffn_w2[layer], fc)
        if layer % 2 == 0:
            w, qg, kg = _even_in_weights(ev_w_in[li], ev_q_norm_g[li], ev_k_norm_g[li])
            qp, k, vaug, qip, kiwb, kiwf, u = _even_proj(xf, g_mix, w, qg, kg, tm)
            attn = _dsa_attention(qp, k, vaug, qip, kiwb, kiwf, batch, seq, tq)
            pool = (u, ev_pool_w[li].astype(BF16), ev_pool_scale[li][None, :])
            xf = _post(xf, attn, ev_w_out[li].astype(BF16), g_ffn, w1, w3, w2, tm, seq, pool=pool)
        else:
            qkv = _odd_proj(xf, g_mix, od_w_qkv[li].astype(BF16), tm)
            attn = _sb_attention(qkv, batch, seq, tq)
            xf = _post(xf, attn, od_w_out[li].astype(BF16), g_ffn, w1, w3, w2, tm, seq)
    return xf.reshape(batch, seq, d)


def kernel(x, norm_mix_g, norm_ffn_g, ev_w_in, ev_w_out, ev_q_norm_g, ev_k_norm_g, ev_pool_w,
           ev_pool_scale, od_w_qkv, od_w_out, ffn_w1, ffn_w3, ffn_w2):
    return _trunk(x, norm_mix_g, norm_ffn_g, ev_w_in, ev_w_out, ev_q_norm_g, ev_k_norm_g, ev_pool_w,
                  ev_pool_scale, od_w_qkv, od_w_out, ffn_w1, ffn_w3, ffn_w2, tm=512, tq=256, fc=256)
```

```python
import functools

import jax
import jax.numpy as jnp
import numpy as np
from jax import lax
from jax.experimental import pallas as pl
from jax.experimental.pallas import tpu as pltpu

F32 = jnp.float32
BF16 = jnp.bfloat16

HEAD_DIM = 64
A_HEADS = 8
A_KV_HEADS = 2
A_REP = A_HEADS // A_KV_HEADS
IDX_HEADS = 4
IDX_DIM = 64
TOPK_MAX = 256
POOL_WINDOWS = (2, 4, 8, 16)
POOL_GROUP = 128
POOL_WIDTH = POOL_GROUP * len(POOL_WINDOWS)
POOL_HALO = 16
EPS = 1e-6

LANES = 128
SUBLANES = 8
KEY_CHUNK = 256
INT_MIN = np.int32(-(2**31))
MASK_BIAS = -1e30
MAX_CONST_SHIFT = 20.0
ROUNDING_SLACK = 1.01
EXP_UNDERFLOW = 105.0
VMEM_LIMIT = 56 * 1024 * 1024

_NT = (((1,), (1,)), ((), ()))


def _dot(a, b):
    return jnp.dot(a, b, preferred_element_type=F32)


def _dot_nt(a, b):
    return lax.dot_general(a, b, _NT, preferred_element_type=F32)


def _split_bf16(x):
    hi = x.astype(BF16)
    lo = (x - hi.astype(F32)).astype(BF16)
    return hi, lo


def _rms_rows(x, g):
    ms = jnp.mean(x * x, axis=-1, keepdims=True)
    return x * lax.rsqrt(ms + EPS) * g


def _fold_rows(x, op):
    parts = [x[r:r + SUBLANES, :] for r in range(0, x.shape[0], SUBLANES)]
    while len(parts) > 1:
        nxt = [op(parts[a], parts[a + 1]) for a in range(0, len(parts) - 1, 2)]
        if len(parts) % 2:
            nxt.append(parts[-1])
        parts = nxt
    return parts[0]


def _const_spec(shape):
    nd = len(shape)
    return pl.BlockSpec(shape, lambda *_: (0,) * nd, pipeline_mode=pl.Buffered(1))


def _params(sem):
    return pltpu.CompilerParams(dimension_semantics=sem, vmem_limit_bytes=VMEM_LIMIT)


QP_W = A_HEADS * LANES
QIP_W = IDX_HEADS * LANES
WT_ROWS = 16
_ROW_V = QP_W
_ROW_QI = _ROW_V + LANES
_ROW_W = _ROW_QI + QIP_W
EVEN_T_ROWS = _ROW_W + WT_ROWS
_COL_KIW = LANES
_COL_U = 2 * LANES
EVEN_COLS = _COL_U + POOL_WIDTH


def _even_proj_kernel(x_ref, g_ref, wc_ref, wt_ref, qg_ref, kg_ref, ones_ref, seg_ref,
                      qpT_ref, k_ref, vaugT_ref, qipT_ref, kiw_ref, wT_ref, u_ref):
    h = _rms_rows(x_ref[...], g_ref[...]).astype(BF16)
    tm = h.shape[0]
    reps = tm // LANES

    for hh in range(A_HEADS):
        rows = slice(hh * LANES, (hh + 1) * LANES)
        yT = _dot_nt(wt_ref[rows, :], h)
        hi, lo = _split_bf16(yT * yT)
        ss = _dot(ones_ref[...], hi) + _dot(ones_ref[...], lo)
        gain = jnp.concatenate([qg_ref[rows, :]] * reps, axis=1)
        qpT_ref[rows, :] = (yT * lax.rsqrt(ss * (1.0 / HEAD_DIM) + EPS) * gain).astype(BF16)

    y = _dot(h, wc_ref[:, 0:LANES])
    hi, lo = _split_bf16(y * y)
    ss = _dot(hi, seg_ref[...]) + _dot(lo, seg_ref[...])
    k_ref[...] = (y * lax.rsqrt(ss * (1.0 / HEAD_DIM) + EPS) * kg_ref[...]).astype(BF16)

    vT = _dot_nt(wt_ref[_ROW_V:_ROW_V + LANES, :], h)
    one = jnp.ones((HEAD_DIM, tm), F32)
    aug = jnp.concatenate([vT[0:HEAD_DIM], one, one, vT[HEAD_DIM:LANES]], axis=0).astype(BF16)
    for j in range(tm // KEY_CHUNK):
        vaugT_ref[j] = aug[:, j * KEY_CHUNK:(j + 1) * KEY_CHUNK]

    qipT_ref[...] = _dot_nt(wt_ref[_ROW_QI:_ROW_QI + QIP_W, :], h).astype(BF16)
    wT_ref[...] = _dot_nt(wt_ref[_ROW_W:_ROW_W + WT_ROWS, :], h)
    kiw_ref[...] = _dot(h, wc_ref[:, _COL_KIW:_COL_KIW + LANES]).astype(BF16)
    u_ref[...] = _dot(h, wc_ref[:, _COL_U:_COL_U + POOL_WIDTH])


def _even_proj(x, g, wc, wt, qg, kg, tm):
    n, d = x.shape
    ones = jnp.ones((LANES, LANES), BF16)
    r = np.arange(LANES) // HEAD_DIM
    seg = jnp.asarray(r[:, None] == r[None, :], BF16)
    row = lambda wd: pl.BlockSpec((tm, wd), lambda i: (i, 0))
    col = lambda ht: pl.BlockSpec((ht, tm), lambda i: (0, i))
    cpt = tm // KEY_CHUNK
    out_specs = [col(QP_W), row(LANES), pl.BlockSpec((cpt, 2 * LANES, KEY_CHUNK), lambda i: (i, 0, 0)),
                 col(QIP_W), row(LANES), col(WT_ROWS), row(POOL_WIDTH)]
    out_shape = [jax.ShapeDtypeStruct((QP_W, n), BF16), jax.ShapeDtypeStruct((n, LANES), BF16),
                 jax.ShapeDtypeStruct((n // KEY_CHUNK, 2 * LANES, KEY_CHUNK), BF16),
                 jax.ShapeDtypeStruct((QIP_W, n), BF16), jax.ShapeDtypeStruct((n, LANES), BF16),
                 jax.ShapeDtypeStruct((WT_ROWS, n), F32), jax.ShapeDtypeStruct((n, POOL_WIDTH), F32)]
    return pl.pallas_call(
        _even_proj_kernel,
        grid=(n // tm,),
        in_specs=[row(d), _const_spec((1, d)), _const_spec(wc.shape), _const_spec(wt.shape),
                  _const_spec(qg.shape), _const_spec((1, LANES)), _const_spec((LANES, LANES)),
                  _const_spec((LANES, LANES))],
        out_specs=out_specs,
        out_shape=out_shape,
        compiler_params=_params(("parallel",)),
        name="even_proj",
    )(x, g, wc, wt, qg, kg, ones, seg)


def _dsa_kernel(shift_ref, qpT_ref, qipT_ref, wT_ref, k_ref, kiw_ref, vaugT_ref, o_ref,
                keys_ref, hi_ref, lo_ref, bias_ref, acc_ref, *, tq, n_sel, pos_bits):
    i = pl.program_id(1)
    nk = i + 1
    kpos = lax.broadcasted_iota(jnp.int32, (KEY_CHUNK, tq), 0)
    qpos = lax.broadcasted_iota(jnp.int32, (KEY_CHUNK, tq), 1)
    causal = kpos <= qpos
    logit_bound = shift_ref[0]
    small_bound = logit_bound <= MAX_CONST_SHIFT
    sel_bias = jnp.where(small_bound, -logit_bound, 0.0)

    def chunk_rows(c):
        return pl.ds(pl.multiple_of(c * KEY_CHUNK, KEY_CHUNK), KEY_CHUNK)

    def score_chunk(c, diagonal):
        kb = kiw_ref[chunk_rows(c), :]
        s = None
        d_next = _dot(kb, qipT_ref[0:LANES, :])
        for hh in range(IDX_HEADS):
            d = d_next
            if hh + 1 < IDX_HEADS:
                d_next = _dot(kb, qipT_ref[(hh + 1) * LANES:(hh + 2) * LANES, :])
            term = jnp.maximum(d, 0.0) * wT_ref[hh:hh + 1, :]
            s = term if s is None else s + term
        bits = lax.bitcast_convert_type(s, jnp.int32)
        key = bits ^ ((bits >> 31) & jnp.int32(0x7FFFFFFF))
        if diagonal:
            key = jnp.where(causal, key, INT_MIN)
        keys_ref[c] = key
        hi_ref[c] = (key >> 16).astype(jnp.int16)
        lo_ref[c] = ((lax.shift_left(key, 16) ^ INT_MIN) >> 16).astype(jnp.int16)

    def score_body(c, carry):
        score_chunk(c, False)
        return carry

    lax.fori_loop(0, i, score_body, 0)
    score_chunk(i, True)

    def count16(ref, pred):
        def body(c, acc):
            m = jnp.where(pred(ref[c]), jnp.int16(1), jnp.int16(0))
            parts = [m[r:r + 2 * SUBLANES, :] for r in range(0, KEY_CHUNK, 2 * SUBLANES)]
            while len(parts) > 1:
                parts = [parts[a] + parts[a + 1] for a in range(0, len(parts), 2)]
            return acc + parts[0]
        acc = lax.fori_loop(0, nk, body, jnp.zeros((2 * SUBLANES, tq), jnp.int16))
        return jnp.sum(acc.astype(jnp.int32), axis=0, keepdims=True)

    def bisect16(ref, target):
        def step(it, thr):
            bit = jnp.where(it == 0, jnp.int32(-32768), lax.shift_left(jnp.int32(1), 15 - it))
            cand = thr ^ bit
            cand16 = cand.astype(jnp.int16)
            cnt = count16(ref, lambda kc: kc >= cand16)
            return jnp.where(cnt >= target, cand, thr)
        return lax.fori_loop(0, 16, step, jnp.full((1, tq), -32768, jnp.int32))

    thr_hi = bisect16(hi_ref, n_sel)
    thr_hi16 = thr_hi.astype(jnp.int16)
    need_lo = n_sel - count16(hi_ref, lambda kc: kc > thr_hi16)

    def bucket_body(c, carry):
        lo_ref[c] = jnp.where(hi_ref[c] == thr_hi16, lo_ref[c], jnp.int16(-32768))
        return carry

    lax.fori_loop(0, nk, bucket_body, 0)
    thr_lo = bisect16(lo_ref, need_lo)
    thr = lax.shift_left(thr_hi, 16) | (thr_lo + 32768)
    thr = jnp.maximum(thr, INT_MIN + 1)

    def count(pred):
        def body(c, acc):
            return acc + _fold_rows(jnp.where(pred(keys_ref[c], c), 1, 0), jnp.add)
        acc = lax.fori_loop(0, nk, body, jnp.zeros((SUBLANES, tq), jnp.int32))
        return jnp.sum(acc, axis=0, keepdims=True)

    cnt_ge = count(lambda kc, c: kc >= thr)
    overflow = jnp.max(jnp.where(cnt_ge > n_sel, 1, 0))

    @pl.when(overflow == 0)
    def _():
        def body(c, carry):
            bias_ref[c] = jnp.where(keys_ref[c] >= thr, sel_bias, MASK_BIAS)
            return carry
        lax.fori_loop(0, nk, body, 0)

    @pl.when(overflow != 0)
    def _():
        need = n_sel - count(lambda kc, c: kc > thr)

        def bisect_pos(it, lim):
            cand = lim | lax.shift_left(jnp.int32(1), pos_bits - 1 - it)
            cnt = count(lambda kc, c: (kc == thr) & (kpos + c * KEY_CHUNK < cand))
            return jnp.where(cnt <= need, cand, lim)

        lim = lax.fori_loop(0, pos_bits, bisect_pos, jnp.zeros((1, tq), jnp.int32))

        def body(c, carry):
            kc = keys_ref[c]
            sel = (kc > thr) | ((kc == thr) & (kpos + c * KEY_CHUNK < lim))
            bias_ref[c] = jnp.where(sel, sel_bias, MASK_BIAS)
            return carry
        lax.fori_loop(0, nk, body, 0)

    acc_ref[...] = jnp.zeros_like(acc_ref)

    def q_rows(hh):
        return qpT_ref[hh * LANES:(hh + 1) * LANES, :]

    def attend(shifts):
        def attn_chunk(c, carry):
            kb = k_ref[chunk_rows(c), :]
            bias = bias_ref[c]
            l_next = _dot(kb, q_rows(0))
            for hh in range(A_HEADS):
                g = hh // A_REP
                l = l_next
                if hh + 1 < A_HEADS:
                    l_next = _dot(kb, q_rows(hh + 1))
                x = l + bias
                if shifts is not None:
                    x = x - shifts[hh]
                p = jnp.exp(x).astype(BF16)
                acc_ref[hh] += _dot(vaugT_ref[c, g * LANES:(g + 1) * LANES, :], p)
            return carry
        lax.fori_loop(0, nk, attn_chunk, 0)

    @pl.when(small_bound)
    def _():
        attend(None)

    @pl.when(jnp.logical_not(small_bound))
    def _():
        def max_chunk(c, maxes):
            kb = k_ref[chunk_rows(c), :]
            bias = bias_ref[c]
            out = []
            for hh in range(A_HEADS):
                l = _dot(kb, q_rows(hh)) + bias
                out.append(jnp.maximum(maxes[hh], _fold_rows(l, jnp.maximum)))
            return tuple(out)
        start = (jnp.full((SUBLANES, tq), MASK_BIAS, F32),) * A_HEADS
        maxes = lax.fori_loop(0, nk, max_chunk, start)
        attend([jnp.max(m, axis=0, keepdims=True) for m in maxes])

    for pr in range(A_HEADS // 2):
        parts = []
        for hh in (2 * pr, 2 * pr + 1):
            a = acc_ref[hh]
            lo, hi = a[0:HEAD_DIM], a[HEAD_DIM:LANES]
            parts.append(lo / hi if hh // A_REP == 0 else hi / lo)
        o_ref[:, pr * LANES:(pr + 1) * LANES] = jnp.concatenate(parts, axis=0).T.astype(BF16)


def _dsa_attention(logit_bound, qpT, k, vaugT, qipT, kiw, wT, batch, seq, tq):
    n = k.shape[0]
    nq = seq // tq
    nc = seq // KEY_CHUNK
    assert tq == KEY_CHUNK
    n_sel = min(TOPK_MAX, seq // 4)
    pos_bits = int(seq).bit_length()
    qcol = lambda ht: pl.BlockSpec((ht, tq), lambda b, i: (0, b * nq + i))
    full = lambda wd: pl.BlockSpec((seq, wd), lambda b, i: (b, 0))
    kern = functools.partial(_dsa_kernel, tq=tq, n_sel=n_sel, pos_bits=pos_bits)
    return pl.pallas_call(
        kern,
        grid=(batch, nq),
        in_specs=[pl.BlockSpec(memory_space=pltpu.SMEM), qcol(QP_W), qcol(QIP_W), qcol(WT_ROWS),
                  full(LANES), full(LANES), pl.BlockSpec((nc, 2 * LANES, KEY_CHUNK), lambda b, i: (b, 0, 0))],
        out_specs=pl.BlockSpec((tq, A_HEADS * HEAD_DIM), lambda b, i: (b * nq + i, 0)),
        out_shape=jax.ShapeDtypeStruct((n, A_HEADS * HEAD_DIM), BF16),
        scratch_shapes=[pltpu.VMEM((nc, KEY_CHUNK, tq), jnp.int32), pltpu.VMEM((nc, KEY_CHUNK, tq), jnp.int16),
                        pltpu.VMEM((nc, KEY_CHUNK, tq), jnp.int16), pltpu.VMEM((nc, KEY_CHUNK, tq), F32),
                        pltpu.VMEM((A_HEADS, LANES, tq), F32)],
        compiler_params=_params(("parallel", "arbitrary")),
        name="dsa_attention",
    )(logit_bound, qpT, qipT, wT, k, kiw, vaugT)


def _odd_proj_kernel(x_ref, g_ref, wk_ref, wt_ref, qT_ref, k_ref, vT_ref):
    h = _rms_rows(x_ref[...], g_ref[...]).astype(BF16)
    tm = h.shape[0]
    width = wk_ref.shape[1]
    step = 512
    for c0 in range(0, width, step):
        k_ref[:, c0:c0 + step] = _dot(h, wk_ref[:, c0:c0 + step]).astype(BF16)
        qT_ref[c0:c0 + step, :] = (_dot_nt(wt_ref[c0:c0 + step, :], h) * (HEAD_DIM ** -0.5)).astype(BF16)
        vT = _dot_nt(wt_ref[width + c0:width + c0 + step, :], h).astype(BF16)
        for j in range(tm // KEY_CHUNK):
            vT_ref[j, c0:c0 + step, :] = vT[:, j * KEY_CHUNK:(j + 1) * KEY_CHUNK]


def _odd_proj(x, g, wk, wt, tm):
    n, d = x.shape
    width = wk.shape[1]
    cpt = tm // KEY_CHUNK
    return pl.pallas_call(
        _odd_proj_kernel,
        grid=(n // tm,),
        in_specs=[pl.BlockSpec((tm, d), lambda i: (i, 0)), _const_spec((1, d)), _const_spec(wk.shape),
                  _const_spec(wt.shape)],
        out_specs=[pl.BlockSpec((width, tm), lambda i: (0, i)), pl.BlockSpec((tm, width), lambda i: (i, 0)),
                   pl.BlockSpec((cpt, width, KEY_CHUNK), lambda i: (i, 0, 0))],
        out_shape=[jax.ShapeDtypeStruct((width, n), BF16), jax.ShapeDtypeStruct((n, width), BF16),
                   jax.ShapeDtypeStruct((n // KEY_CHUNK, width, KEY_CHUNK), BF16)],
        compiler_params=_params(("parallel",)),
        name="odd_proj",
    )(x, g, wk, wt)


def _sb_kernel(qT_ref, k_ref, vT_ref, tri_ref, o_ref, acc_ref, *, tq):
    i = pl.program_id(2)
    kpos = lax.broadcasted_iota(jnp.int32, (KEY_CHUNK, tq), 0)
    qpos = lax.broadcasted_iota(jnp.int32, (KEY_CHUNK, tq), 1)
    strict = kpos < qpos
    top_rows = lax.broadcasted_iota(jnp.int32, (LANES, tq), 0) < HEAD_DIM
    qT = qT_ref[...]
    zero = jnp.zeros_like(qT)
    qTs = (jnp.where(top_rows, qT, zero), jnp.where(top_rows, zero, qT))
    acc_ref[...] = jnp.zeros_like(acc_ref)

    def block(j, diagonal, carries):
        kb = k_ref[pl.ds(pl.multiple_of(j * KEY_CHUNK, KEY_CHUNK), KEY_CHUNK), :]
        vb = vT_ref[j]
        out = []
        for hh in range(2):
            z = _dot(kb, qTs[hh])
            l1p = jnp.log(1.0 + jnp.exp(-jnp.abs(z)))
            sp = jnp.maximum(z, 0.0) + l1p
            ls = jnp.minimum(z, 0.0) - l1p
            if diagonal:
                sp = jnp.where(strict, sp, 0.0)
            hi, lo = _split_bf16(sp)
            later = _dot(tri_ref[...], hi) + _dot(tri_ref[...], lo)
            a = jnp.exp(ls - later - carries[hh])
            if diagonal:
                a = jnp.where(strict, a, 0.0)
            acc_ref[hh] += _dot(vb, a.astype(BF16))
            out.append(carries[hh] + jnp.sum(_fold_rows(sp, jnp.add), axis=0, keepdims=True))
        return tuple(out)

    def unfinished(carries):
        low = jnp.minimum(jnp.min(carries[0]), jnp.min(carries[1]))
        return (low <= EXP_UNDERFLOW).astype(jnp.int32)

    zero_row = jnp.zeros((1, tq), F32)
    carries = block(i, True, (zero_row, zero_row))

    def cond(st):
        return jnp.logical_and(st[0] >= 0, st[1] > 0)

    def body(st):
        carries = block(st[0], False, st[2])
        return st[0] - 1, unfinished(carries), carries

    lax.while_loop(cond, body, (i - 1, unfinished(carries), carries))
    oT = jnp.where(top_rows, acc_ref[0], acc_ref[1])
    o_ref[...] = oT.T.astype(BF16)


def _sb_attention(qT, k, vT, batch, seq, tq):
    n, width = k.shape
    pairs = width // LANES
    nq = seq // tq
    nc = seq // KEY_CHUNK
    assert tq == KEY_CHUNK
    r = np.arange(KEY_CHUNK)
    tri = jnp.asarray(r[None, :] > r[:, None], BF16)
    kern = functools.partial(_sb_kernel, tq=tq)
    return pl.pallas_call(
        kern,
        grid=(batch, pairs, nq),
        in_specs=[pl.BlockSpec((LANES, tq), lambda b, p, i: (p, b * nq + i)),
                  pl.BlockSpec((seq, LANES), lambda b, p, i: (b, p)),
                  pl.BlockSpec((nc, LANES, KEY_CHUNK), lambda b, p, i: (b, p, 0)),
                  _const_spec((KEY_CHUNK, KEY_CHUNK))],
        out_specs=pl.BlockSpec((tq, LANES), lambda b, p, i: (b * nq + i, p)),
        out_shape=jax.ShapeDtypeStruct((n, width), BF16),
        scratch_shapes=[pltpu.VMEM((2, LANES, tq), F32)],
        compiler_params=_params(("parallel", "parallel", "arbitrary")),
        name="sb_attention",
    )(qT, k, vT, tri)


def _pooled(u_ref, halo_ref, pw_ref, ps_ref, ext_ref, seq, tm):
    tpos0 = (pl.program_id(0) * tm) % seq
    halo = jnp.where(tpos0 == 0, 0.0, halo_ref[...])
    ext_ref[0:POOL_HALO, :] = halo
    ext_ref[POOL_HALO:POOL_HALO + tm, :] = u_ref[...]
    t_idx = lax.broadcasted_iota(jnp.int32, (tm, POOL_GROUP), 0) + tpos0
    out = []
    for gi, w in enumerate(POOL_WINDOWS):
        sl = slice(gi * POOL_GROUP, (gi + 1) * POOL_GROUP)
        s = ext_ref[POOL_HALO:POOL_HALO + tm, sl]
        for dlt in range(1, w):
            s = s + ext_ref[POOL_HALO - dlt:POOL_HALO - dlt + tm, sl]
        cnt = jnp.minimum(t_idx + 1, w).astype(F32)
        mixed = s / cnt - u_ref[:, sl]
        y = _dot(mixed.astype(BF16), pw_ref[gi]) * ps_ref[:, sl]
        out.append(y.astype(BF16))
    return out


def _post_kernel(*refs, even, seq, tm):
    if even:
        (attn_ref, u_ref, halo_ref, pw_ref, ps_ref, wo_ref, x_ref, g_ref, w1_ref, w3_ref, w2_ref,
         o_ref, acc_ref, ext_ref) = refs
        half = attn_ref.shape[1]
        y = _dot(attn_ref[...], wo_ref[0:half, :])
        for gi, pg in enumerate(_pooled(u_ref, halo_ref, pw_ref, ps_ref, ext_ref, seq, tm)):
            y = y + _dot(pg, wo_ref[half + gi * POOL_GROUP:half + (gi + 1) * POOL_GROUP, :])
    else:
        attn_ref, wo_ref, x_ref, g_ref, w1_ref, w3_ref, w2_ref, o_ref, acc_ref = refs
        y = _dot(attn_ref[...], wo_ref[...])
    x1 = x_ref[...] + y
    h = _rms_rows(x1, g_ref[...]).astype(BF16)
    acc_ref[...] = x1

    def ffn_chunk(c, carry):
        a = _dot(h, w1_ref[c])
        b = _dot(h, w3_ref[c])
        gate = (a * (1.0 / (1.0 + jnp.exp(-a))) * b).astype(BF16)
        acc_ref[...] += _dot(gate, w2_ref[c])
        return carry

    lax.fori_loop(0, w1_ref.shape[0], ffn_chunk, 0)
    o_ref[...] = acc_ref[...]


def _post(x, attn, w_out, g, w1, w3, w2, tm, seq, pool=None):
    n, d = x.shape
    even = pool is not None
    row = lambda wd: pl.BlockSpec((tm, wd), lambda i: (i, 0))
    ffn_specs = [_const_spec(w_out.shape), row(d), _const_spec((1, d)),
                 _const_spec(w1.shape), _const_spec(w3.shape), _const_spec(w2.shape)]
    scratch = [pltpu.VMEM((tm, d), F32)]
    if even:
        u, pool_w, pool_scale = pool
        halo_blocks = tm // POOL_HALO
        halo = pl.BlockSpec((POOL_HALO, u.shape[1]), lambda i: (jnp.maximum(i * halo_blocks - 1, 0), 0))
        in_specs = [row(attn.shape[1]), row(u.shape[1]), halo, _const_spec(pool_w.shape),
                    _const_spec(pool_scale.shape)] + ffn_specs
        args = (attn, u, u, pool_w, pool_scale, w_out, x, g, w1, w3, w2)
        scratch.append(pltpu.VMEM((POOL_HALO + tm, u.shape[1]), F32))
    else:
        in_specs = [row(attn.shape[1])] + ffn_specs
        args = (attn, w_out, x, g, w1, w3, w2)
    kern = functools.partial(_post_kernel, even=even, seq=seq, tm=tm)
    return pl.pallas_call(
        kern,
        grid=(n // tm,),
        in_specs=in_specs,
        out_specs=row(d),
        out_shape=jax.ShapeDtypeStruct((n, d), F32),
        scratch_shapes=scratch,
        compiler_params=_params(("parallel",)),
        name="post_even" if even else "post_odd",
    )(*args)


def _even_in_weights(w_in, q_g, k_g):
    d = w_in.shape[0]
    a_w, kv_w, qi_w = A_HEADS * HEAD_DIM, A_KV_HEADS * HEAD_DIM, IDX_HEADS * IDX_DIM
    o_k, o_v, o_qi = a_w, a_w + kv_w, a_w + 2 * kv_w
    o_ki = o_qi + qi_w
    o_wi = o_ki + IDX_DIM
    o_u = o_wi + IDX_HEADS
    idx_scale = (IDX_HEADS * IDX_DIM) ** -0.5
    w_idx = w_in[:, o_wi:o_u] * idx_scale
    zeros = lambda wd: jnp.zeros((d, wd), w_in.dtype)
    gq = q_g * (HEAD_DIM ** -0.5)
    gz = jnp.zeros_like(gq)
    t_cols, qg = [], []
    for hh in range(A_HEADS):
        wq = w_in[:, hh * HEAD_DIM:(hh + 1) * HEAD_DIM]
        first = hh // A_REP == 0
        t_cols += [wq, zeros(HEAD_DIM)] if first else [zeros(HEAD_DIM), wq]
        qg += [gq, gz] if first else [gz, gq]
    t_cols += [w_in[:, o_v:o_qi]]
    for hh in range(IDX_HEADS):
        t_cols += [w_in[:, o_qi + hh * IDX_DIM:o_qi + (hh + 1) * IDX_DIM], zeros(LANES - IDX_DIM)]
    t_cols += [w_idx, zeros(WT_ROWS - IDX_HEADS)]
    wt = jnp.concatenate(t_cols, axis=1).T.astype(BF16)
    wc = jnp.concatenate([w_in[:, o_k:o_v], w_in[:, o_ki:o_wi], w_idx, zeros(LANES - IDX_DIM - IDX_HEADS),
                          w_in[:, o_u:]], axis=1).astype(BF16)
    qg_rows = jnp.broadcast_to(jnp.concatenate(qg)[:, None], (QP_W, LANES))
    return wc, wt, qg_rows, jnp.tile(k_g, A_KV_HEADS)[None, :]


def _odd_in_weights(w_qkv):
    width = w_qkv.shape[1] // 3
    wk = w_qkv[:, width:2 * width].astype(BF16)
    wt = jnp.concatenate([w_qkv[:, :width], w_qkv[:, 2 * width:]], axis=1).T.astype(BF16)
    return wk, wt


def _ffn_weights(w1, w3, w2, fc):
    d, f = w1.shape
    nc = f // fc
    w1c = w1.reshape(d, nc, fc).transpose(1, 0, 2).astype(BF16)
    w3c = w3.reshape(d, nc, fc).transpose(1, 0, 2).astype(BF16)
    w2c = w2.reshape(nc, fc, d).astype(BF16)
    return w1c, w3c, w2c


def _trunk(x, norm_mix_g, norm_ffn_g, ev_w_in, ev_w_out, ev_q_norm_g, ev_k_norm_g, ev_pool_w,
           ev_pool_scale, od_w_qkv, od_w_out, ffn_w1, ffn_w3, ffn_w2, *, tm, tq, fc):
    batch, seq, d = x.shape
    depth = norm_mix_g.shape[0]
    xf = x.reshape(batch * seq, d)
    for layer in range(depth):
        li = layer // 2
        g_mix = norm_mix_g[layer][None, :]
        g_ffn = norm_ffn_g[layer][None, :]
        w1, w3, w2 = _ffn_weights(ffn_w1[layer], ffn_w3[layer], ffn_w2[layer], fc)
        if layer % 2 == 0:
            wc, wt, qg, kg = _even_in_weights(ev_w_in[li], ev_q_norm_g[li], ev_k_norm_g[li])
            qpT, k, vaugT, qipT, kiw, wT, u = _even_proj(xf, g_mix, wc, wt, qg, kg, tm)
            bound = (HEAD_DIM ** 0.5) * jnp.max(jnp.abs(ev_q_norm_g[li])) * jnp.max(jnp.abs(ev_k_norm_g[li]))
            bound = (ROUNDING_SLACK * bound).astype(F32).reshape(1)
            attn = _dsa_attention(bound, qpT, k, vaugT, qipT, kiw, wT, batch, seq, tq)
            pool = (u, ev_pool_w[li].astype(BF16), ev_pool_scale[li][None, :])
            xf = _post(xf, attn, ev_w_out[li].astype(BF16), g_ffn, w1, w3, w2, tm, seq, pool=pool)
        else:
            wk, wt = _odd_in_weights(od_w_qkv[li])
            qT, k, vT = _odd_proj(xf, g_mix, wk, wt, tm)
            attn = _sb_attention(qT, k, vT, batch, seq, tq)
            xf = _post(xf, attn, od_w_out[li].astype(BF16), g_ffn, w1, w3, w2, tm, seq)
    return xf.reshape(batch, seq, d)


def kernel(x, norm_mix_g, norm_ffn_g, ev_w_in, ev_w_out, ev_q_norm_g, ev_k_norm_g, ev_pool_w,
           ev_pool_scale, od_w_qkv, od_w_out, ffn_w1, ffn_w3, ffn_w2):
    return _trunk(x, norm_mix_g, norm_ffn_g, ev_w_in, ev_w_out, ev_q_norm_g, ev_k_norm_g, ev_pool_w,
                  ev_pool_scale, od_w_qkv, od_w_out, ffn_w1, ffn_w3, ffn_w2, tm=512, tq=KEY_CHUNK, fc=256)
```

```python
import functools

import jax
import jax.numpy as jnp
import numpy as np
from jax import lax
from jax.experimental import pallas as pl
from jax.experimental.pallas import tpu as pltpu

F32 = jnp.float32
BF16 = jnp.bfloat16

HEAD_DIM = 64
A_HEADS = 8
A_KV_HEADS = 2
A_REP = A_HEADS // A_KV_HEADS
IDX_HEADS = 4
IDX_DIM = 64
TOPK_MAX = 256
POOL_WINDOWS = (2, 4, 8, 16)
POOL_GROUP = 128
POOL_WIDTH = POOL_GROUP * len(POOL_WINDOWS)
POOL_HALO = 16
EPS = 1e-6

LANES = 128
SUBLANES = 8
KEY_CHUNK = 256
INT_MIN = np.int32(-(2**31))
MASK_BIAS = -1e30
MAX_CONST_SHIFT = 20.0
ROUNDING_SLACK = 1.01
EXP_UNDERFLOW = 105.0
VMEM_LIMIT = 56 * 1024 * 1024

_NT = (((1,), (1,)), ((), ()))


def _dot(a, b):
    return jnp.dot(a, b, preferred_element_type=F32)


def _dot_nt(a, b):
    return lax.dot_general(a, b, _NT, preferred_element_type=F32)


def _split_bf16(x):
    hi = x.astype(BF16)
    lo = (x - hi.astype(F32)).astype(BF16)
    return hi, lo


def _rms_rows(x, g):
    ms = jnp.mean(x * x, axis=-1, keepdims=True)
    return x * lax.rsqrt(ms + EPS) * g


def _fold_rows(x, op):
    parts = [x[r:r + SUBLANES, :] for r in range(0, x.shape[0], SUBLANES)]
    while len(parts) > 1:
        nxt = [op(parts[a], parts[a + 1]) for a in range(0, len(parts) - 1, 2)]
        if len(parts) % 2:
            nxt.append(parts[-1])
        parts = nxt
    return parts[0]


def _const_spec(shape):
    nd = len(shape)
    return pl.BlockSpec(shape, lambda *_: (0,) * nd, pipeline_mode=pl.Buffered(1))


def _params(sem):
    return pltpu.CompilerParams(dimension_semantics=sem, vmem_limit_bytes=VMEM_LIMIT)


QB = KEY_CHUNK
QP_W = A_HEADS * LANES
QIP_W = IDX_HEADS * LANES
WT_ROWS = 16
_ROW_V = QP_W
_ROW_QI = _ROW_V + LANES
_ROW_W = _ROW_QI + QIP_W
EVEN_T_ROWS = _ROW_W + WT_ROWS
_COL_KIW = LANES
_COL_U = 2 * LANES
EVEN_COLS = _COL_U + POOL_WIDTH


def _even_proj_kernel(x_ref, g_ref, wc_ref, wt_ref, qg_ref, kg_ref, ones_ref, seg_ref,
                      qpT_ref, k_ref, vaugT_ref, qipT_ref, kiw_ref, wrow_ref, u_ref):
    h = _rms_rows(x_ref[...], g_ref[...]).astype(BF16)
    tm = h.shape[0]
    reps = tm // LANES
    blocks = tm // QB

    for hh in range(A_HEADS):
        rows = slice(hh * LANES, (hh + 1) * LANES)
        yT = _dot_nt(wt_ref[rows, :], h)
        hi, lo = _split_bf16(yT * yT)
        ss = _dot(ones_ref[...], hi) + _dot(ones_ref[...], lo)
        gain = jnp.concatenate([qg_ref[rows, :]] * reps, axis=1)
        qT = (yT * lax.rsqrt(ss * (1.0 / HEAD_DIM) + EPS) * gain).astype(BF16)
        g, r = divmod(hh, A_REP)
        for j in range(blocks):
            qpT_ref[j, g, :, r * QB:(r + 1) * QB] = qT[:, j * QB:(j + 1) * QB]

    y = _dot(h, wc_ref[:, 0:LANES])
    hi, lo = _split_bf16(y * y)
    ss = _dot(hi, seg_ref[...]) + _dot(lo, seg_ref[...])
    k_ref[...] = (y * lax.rsqrt(ss * (1.0 / HEAD_DIM) + EPS) * kg_ref[...]).astype(BF16)

    vT = _dot_nt(wt_ref[_ROW_V:_ROW_V + LANES, :], h)
    one = jnp.ones((HEAD_DIM, tm), F32)
    aug = jnp.concatenate([vT[0:HEAD_DIM], one, one, vT[HEAD_DIM:LANES]], axis=0).astype(BF16)
    for j in range(tm // KEY_CHUNK):
        vaugT_ref[j] = aug[:, j * KEY_CHUNK:(j + 1) * KEY_CHUNK]

    qiT = _dot_nt(wt_ref[_ROW_QI:_ROW_QI + QIP_W, :], h).astype(BF16)
    wT = _dot_nt(wt_ref[_ROW_W:_ROW_W + WT_ROWS, :], h)
    for hh in range(IDX_HEADS):
        for j in range(blocks):
            qipT_ref[j, :, hh * QB:(hh + 1) * QB] = qiT[hh * LANES:(hh + 1) * LANES, j * QB:(j + 1) * QB]
            wrow_ref[j, :, hh * QB:(hh + 1) * QB] = jnp.broadcast_to(
                wT[hh:hh + 1, j * QB:(j + 1) * QB], (SUBLANES, QB))
    kiw_ref[...] = _dot(h, wc_ref[:, _COL_KIW:_COL_KIW + LANES]).astype(BF16)
    u_ref[...] = _dot(h, wc_ref[:, _COL_U:_COL_U + POOL_WIDTH])


def _even_proj(x, g, wc, wt, qg, kg, tm):
    n, d = x.shape
    ones = jnp.ones((LANES, LANES), BF16)
    r = np.arange(LANES) // HEAD_DIM
    seg = jnp.asarray(r[:, None] == r[None, :], BF16)
    row = lambda wd: pl.BlockSpec((tm, wd), lambda i: (i, 0))
    cpt = tm // KEY_CHUNK
    bpt = tm // QB
    out_specs = [pl.BlockSpec((bpt, A_KV_HEADS, LANES, A_REP * QB), lambda i: (i, 0, 0, 0)), row(LANES),
                 pl.BlockSpec((cpt, 2 * LANES, KEY_CHUNK), lambda i: (i, 0, 0)),
                 pl.BlockSpec((bpt, LANES, IDX_HEADS * QB), lambda i: (i, 0, 0)), row(LANES),
                 pl.BlockSpec((bpt, SUBLANES, IDX_HEADS * QB), lambda i: (i, 0, 0)), row(POOL_WIDTH)]
    out_shape = [jax.ShapeDtypeStruct((n // QB, A_KV_HEADS, LANES, A_REP * QB), BF16),
                 jax.ShapeDtypeStruct((n, LANES), BF16),
                 jax.ShapeDtypeStruct((n // KEY_CHUNK, 2 * LANES, KEY_CHUNK), BF16),
                 jax.ShapeDtypeStruct((n // QB, LANES, IDX_HEADS * QB), BF16),
                 jax.ShapeDtypeStruct((n, LANES), BF16),
                 jax.ShapeDtypeStruct((n // QB, SUBLANES, IDX_HEADS * QB), F32),
                 jax.ShapeDtypeStruct((n, POOL_WIDTH), F32)]
    return pl.pallas_call(
        _even_proj_kernel,
        grid=(n // tm,),
        in_specs=[row(d), _const_spec((1, d)), _const_spec(wc.shape), _const_spec(wt.shape),
                  _const_spec(qg.shape), _const_spec((1, LANES)), _const_spec((LANES, LANES)),
                  _const_spec((LANES, LANES))],
        out_specs=out_specs,
        out_shape=out_shape,
        compiler_params=_params(("parallel",)),
        name="even_proj",
    )(x, g, wc, wt, qg, kg, ones, seg)


def _dsa_kernel(shift_ref, qpT_ref, qipT_ref, wrow_ref, k_ref, kiw_ref, vaugT_ref, o_ref,
                keys_ref, hi_ref, lo_ref, bias_ref, acc_ref, *, n_sel, pos_bits):
    i = pl.program_id(1)
    nk = i + 1
    kpos = lax.broadcasted_iota(jnp.int32, (KEY_CHUNK, QB), 0)
    qpos = lax.broadcasted_iota(jnp.int32, (KEY_CHUNK, QB), 1)
    causal = kpos <= qpos
    logit_bound = shift_ref[0]
    small_bound = logit_bound <= MAX_CONST_SHIFT
    sel_bias = jnp.where(small_bound, -logit_bound, 0.0)

    def chunk_rows(c):
        return pl.ds(pl.multiple_of(c * KEY_CHUNK, KEY_CHUNK), KEY_CHUNK)

    def per_head(x, n):
        return [x[:, r * QB:(r + 1) * QB] for r in range(n)]

    def score_chunk(c, diagonal):
        d = _dot(kiw_ref[chunk_rows(c), :], qipT_ref[0])
        terms = per_head(jnp.maximum(d, 0.0) * wrow_ref[0, 0:1, :], IDX_HEADS)
        s = (terms[0] + terms[1]) + (terms[2] + terms[3])
        bits = lax.bitcast_convert_type(s, jnp.int32)
        key = bits ^ ((bits >> 31) & jnp.int32(0x7FFFFFFF))
        if diagonal:
            key = jnp.where(causal, key, INT_MIN)
        keys_ref[c] = key
        hi_ref[c] = (key >> 16).astype(jnp.int16)
        lo_ref[c] = ((key & 0xFFFF) - 32768).astype(jnp.int16)

    def score_body(c, carry):
        score_chunk(c, False)
        return carry

    lax.fori_loop(0, i, score_body, 0)
    score_chunk(i, True)

    def count16(ref, pred):
        def body(c, acc):
            m = jnp.where(pred(ref[c]), jnp.int16(1), jnp.int16(0))
            parts = [m[r:r + 2 * SUBLANES, :] for r in range(0, KEY_CHUNK, 2 * SUBLANES)]
            while len(parts) > 1:
                parts = [parts[a] + parts[a + 1] for a in range(0, len(parts), 2)]
            return acc + parts[0]
        acc = lax.fori_loop(0, nk, body, jnp.zeros((2 * SUBLANES, QB), jnp.int16))
        return jnp.sum(acc.astype(jnp.int32), axis=0, keepdims=True)

    def bisect16(ref, target):
        def step(it, thr):
            bit = jnp.where(it == 0, jnp.int32(-32768), lax.shift_left(jnp.int32(1), 15 - it))
            cand = thr ^ bit
            cand16 = cand.astype(jnp.int16)
            cnt = count16(ref, lambda kc: kc >= cand16)
            return jnp.where(cnt >= target, cand, thr)
        return lax.fori_loop(0, 16, step, jnp.full((1, QB), -32768, jnp.int32))

    thr_hi = bisect16(hi_ref, n_sel)
    thr_hi16 = thr_hi.astype(jnp.int16)
    need_lo = n_sel - count16(hi_ref, lambda kc: kc > thr_hi16)

    def bucket_body(c, carry):
        lo_ref[c] = jnp.where(hi_ref[c] == thr_hi16, lo_ref[c], jnp.int16(-32768))
        return carry

    lax.fori_loop(0, nk, bucket_body, 0)
    thr_lo = bisect16(lo_ref, need_lo)
    thr = lax.shift_left(thr_hi, 16) | (thr_lo + 32768)
    thr = jnp.maximum(thr, INT_MIN + 1)

    def count(pred):
        def body(c, acc):
            return acc + _fold_rows(jnp.where(pred(keys_ref[c], c), 1, 0), jnp.add)
        acc = lax.fori_loop(0, nk, body, jnp.zeros((SUBLANES, QB), jnp.int32))
        return jnp.sum(acc, axis=0, keepdims=True)

    cnt_ge = count(lambda kc, c: kc >= thr)
    overflow = jnp.max(jnp.where(cnt_ge > n_sel, 1, 0))

    @pl.when(overflow == 0)
    def _():
        def body(c, carry):
            bias_ref[c] = jnp.where(keys_ref[c] >= thr, sel_bias, MASK_BIAS)
            return carry
        lax.fori_loop(0, nk, body, 0)

    @pl.when(overflow != 0)
    def _():
        need = n_sel - count(lambda kc, c: kc > thr)

        def bisect_pos(it, lim):
            cand = lim | lax.shift_left(jnp.int32(1), pos_bits - 1 - it)
            cnt = count(lambda kc, c: (kc == thr) & (kpos + c * KEY_CHUNK < cand))
            return jnp.where(cnt <= need, cand, lim)

        lim = lax.fori_loop(0, pos_bits, bisect_pos, jnp.zeros((1, QB), jnp.int32))

        def body(c, carry):
            kc = keys_ref[c]
            sel = (kc > thr) | ((kc == thr) & (kpos + c * KEY_CHUNK < lim))
            bias_ref[c] = jnp.where(sel, sel_bias, MASK_BIAS)
            return carry
        lax.fori_loop(0, nk, body, 0)

    acc_ref[...] = jnp.zeros_like(acc_ref)

    def group_logits(c, g):
        bias = bias_ref[c]
        return _dot(k_ref[chunk_rows(c), :], qpT_ref[0, g]) + jnp.concatenate([bias] * A_REP, axis=1)

    def attend(shifts):
        def attn_chunk(c, carry):
            for g in range(A_KV_HEADS):
                x = group_logits(c, g)
                if shifts is not None:
                    x = x - shifts[g]
                p = jnp.exp(x).astype(BF16)
                acc_ref[g] += _dot(vaugT_ref[c, g * LANES:(g + 1) * LANES, :], p)
            return carry
        lax.fori_loop(0, nk, attn_chunk, 0)

    @pl.when(small_bound)
    def _():
        attend(None)

    @pl.when(jnp.logical_not(small_bound))
    def _():
        def max_chunk(c, maxes):
            return tuple(jnp.maximum(maxes[g], _fold_rows(group_logits(c, g), jnp.maximum))
                         for g in range(A_KV_HEADS))
        start = (jnp.full((SUBLANES, A_REP * QB), MASK_BIAS, F32),) * A_KV_HEADS
        maxes = lax.fori_loop(0, nk, max_chunk, start)
        attend([jnp.max(m, axis=0, keepdims=True) for m in maxes])

    for pr in range(A_HEADS // 2):
        parts = []
        for hh in (2 * pr, 2 * pr + 1):
            g, r = divmod(hh, A_REP)
            a = acc_ref[g, :, r * QB:(r + 1) * QB]
            lo, hi = a[0:HEAD_DIM], a[HEAD_DIM:LANES]
            parts.append(lo / hi if g == 0 else hi / lo)
        o_ref[:, pr * LANES:(pr + 1) * LANES] = jnp.concatenate(parts, axis=0).T.astype(BF16)


def _dsa_attention(logit_bound, qpT, k, vaugT, qipT, kiw, wrow, batch, seq):
    n = k.shape[0]
    nq = seq // QB
    nc = seq // KEY_CHUNK
    n_sel = min(TOPK_MAX, seq // 4)
    pos_bits = int(seq).bit_length()
    full = lambda wd: pl.BlockSpec((seq, wd), lambda b, i: (b, 0))
    kern = functools.partial(_dsa_kernel, n_sel=n_sel, pos_bits=pos_bits)
    return pl.pallas_call(
        kern,
        grid=(batch, nq),
        in_specs=[pl.BlockSpec(memory_space=pltpu.SMEM),
                  pl.BlockSpec((1, A_KV_HEADS, LANES, A_REP * QB), lambda b, i: (b * nq + i, 0, 0, 0)),
                  pl.BlockSpec((1, LANES, IDX_HEADS * QB), lambda b, i: (b * nq + i, 0, 0)),
                  pl.BlockSpec((1, SUBLANES, IDX_HEADS * QB), lambda b, i: (b * nq + i, 0, 0)),
                  full(LANES), full(LANES), pl.BlockSpec((nc, 2 * LANES, KEY_CHUNK), lambda b, i: (b, 0, 0))],
        out_specs=pl.BlockSpec((QB, A_HEADS * HEAD_DIM), lambda b, i: (b * nq + i, 0)),
        out_shape=jax.ShapeDtypeStruct((n, A_HEADS * HEAD_DIM), BF16),
        scratch_shapes=[pltpu.VMEM((nc, KEY_CHUNK, QB), jnp.int32), pltpu.VMEM((nc, KEY_CHUNK, QB), jnp.int16),
                        pltpu.VMEM((nc, KEY_CHUNK, QB), jnp.int16), pltpu.VMEM((nc, KEY_CHUNK, QB), F32),
                        pltpu.VMEM((A_KV_HEADS, LANES, A_REP * QB), F32)],
        compiler_params=_params(("parallel", "arbitrary")),
        name="dsa_attention",
    )(logit_bound, qpT, qipT, wrow, k, kiw, vaugT)


def _odd_proj_kernel(x_ref, g_ref, wk_ref, wt_ref, qT_ref, k_ref, vT_ref):
    h = _rms_rows(x_ref[...], g_ref[...]).astype(BF16)
    tm = h.shape[0]
    width = wk_ref.shape[1]
    step = 512
    top_rows = lax.broadcasted_iota(jnp.int32, (LANES, tm), 0) < HEAD_DIM
    for c0 in range(0, width, step):
        k_ref[:, c0:c0 + step] = _dot(h, wk_ref[:, c0:c0 + step]).astype(BF16)
        qT = _dot_nt(wt_ref[c0:c0 + step, :], h) * (HEAD_DIM ** -0.5)
        for pp in range(step // LANES):
            y = qT[pp * LANES:(pp + 1) * LANES, :]
            first = jnp.where(top_rows, y, 0.0).astype(BF16)
            second = jnp.where(top_rows, 0.0, y).astype(BF16)
            for j in range(tm // QB):
                qT_ref[j, c0 // LANES + pp, :, 0:QB] = first[:, j * QB:(j + 1) * QB]
                qT_ref[j, c0 // LANES + pp, :, QB:2 * QB] = second[:, j * QB:(j + 1) * QB]
        vT = _dot_nt(wt_ref[width + c0:width + c0 + step, :], h).astype(BF16)
        for j in range(tm // KEY_CHUNK):
            vT_ref[j, c0:c0 + step, :] = vT[:, j * KEY_CHUNK:(j + 1) * KEY_CHUNK]


def _odd_proj(x, g, wk, wt, tm):
    n, d = x.shape
    width = wk.shape[1]
    pairs = width // LANES
    return pl.pallas_call(
        _odd_proj_kernel,
        grid=(n // tm,),
        in_specs=[pl.BlockSpec((tm, d), lambda i: (i, 0)), _const_spec((1, d)), _const_spec(wk.shape),
                  _const_spec(wt.shape)],
        out_specs=[pl.BlockSpec((tm // QB, pairs, LANES, 2 * QB), lambda i: (i, 0, 0, 0)),
                   pl.BlockSpec((tm, width), lambda i: (i, 0)),
                   pl.BlockSpec((tm // KEY_CHUNK, width, KEY_CHUNK), lambda i: (i, 0, 0))],
        out_shape=[jax.ShapeDtypeStruct((n // QB, pairs, LANES, 2 * QB), BF16),
                   jax.ShapeDtypeStruct((n, width), BF16),
                   jax.ShapeDtypeStruct((n // KEY_CHUNK, width, KEY_CHUNK), BF16)],
        compiler_params=_params(("parallel",)),
        name="odd_proj",
    )(x, g, wk, wt)


def _sb_kernel(qT_ref, k_ref, vT_ref, tri_ref, o_ref, acc_ref):
    i = pl.program_id(2)
    kpos = lax.broadcasted_iota(jnp.int32, (KEY_CHUNK, 2 * QB), 0)
    qpos = lax.broadcasted_iota(jnp.int32, (KEY_CHUNK, 2 * QB), 1) & (QB - 1)
    strict = kpos < qpos
    acc_ref[...] = jnp.zeros_like(acc_ref)

    def block(j, diagonal, carry):
        kb = k_ref[pl.ds(pl.multiple_of(j * KEY_CHUNK, KEY_CHUNK), KEY_CHUNK), :]
        z = _dot(kb, qT_ref[0, 0])
        l1p = jnp.log(1.0 + jnp.exp(-jnp.abs(z)))
        sp = jnp.maximum(z, 0.0) + l1p
        ls = jnp.minimum(z, 0.0) - l1p
        if diagonal:
            sp = jnp.where(strict, sp, 0.0)
        later = _dot(tri_ref[...], sp.astype(BF16))
        a = jnp.exp(ls - later - carry)
        if diagonal:
            a = jnp.where(strict, a, 0.0)
        acc_ref[...] += _dot(vT_ref[j], a.astype(BF16))
        return carry + jnp.sum(_fold_rows(sp, jnp.add), axis=0, keepdims=True)

    def unfinished(carry):
        return (jnp.min(carry) <= EXP_UNDERFLOW).astype(jnp.int32)

    carry = block(i, True, jnp.zeros((1, 2 * QB), F32))

    def cond(st):
        return jnp.logical_and(st[0] >= 0, st[1] > 0)

    def body(st):
        carry = block(st[0], False, st[2])
        return st[0] - 1, unfinished(carry), carry

    lax.while_loop(cond, body, (i - 1, unfinished(carry), carry))
    oT = jnp.concatenate([acc_ref[0:HEAD_DIM, 0:QB], acc_ref[HEAD_DIM:LANES, QB:2 * QB]], axis=0)
    o_ref[...] = oT.T.astype(BF16)


def _sb_attention(qT, k, vT, batch, seq):
    n, width = k.shape
    pairs = width // LANES
    nq = seq // QB
    nc = seq // KEY_CHUNK
    r = np.arange(KEY_CHUNK)
    tri = jnp.asarray(r[None, :] > r[:, None], BF16)
    return pl.pallas_call(
        _sb_kernel,
        grid=(batch, pairs, nq),
        in_specs=[pl.BlockSpec((1, 1, LANES, 2 * QB), lambda b, p, i: (b * nq + i, p, 0, 0)),
                  pl.BlockSpec((seq, LANES), lambda b, p, i: (b, p)),
                  pl.BlockSpec((nc, LANES, KEY_CHUNK), lambda b, p, i: (b, p, 0)),
                  _const_spec((KEY_CHUNK, KEY_CHUNK))],
        out_specs=pl.BlockSpec((QB, LANES), lambda b, p, i: (b * nq + i, p)),
        out_shape=jax.ShapeDtypeStruct((n, width), BF16),
        scratch_shapes=[pltpu.VMEM((LANES, 2 * QB), F32)],
        compiler_params=_params(("parallel", "parallel", "arbitrary")),
        name="sb_attention",
    )(qT, k, vT, tri)


def _pooled(u_ref, halo_ref, pw_ref, ps_ref, ext_ref, seq, tm):
    tpos0 = (pl.program_id(0) * tm) % seq
    halo = jnp.where(tpos0 == 0, 0.0, halo_ref[...])
    ext_ref[0:POOL_HALO, :] = halo
    ext_ref[POOL_HALO:POOL_HALO + tm, :] = u_ref[...]
    t_idx = lax.broadcasted_iota(jnp.int32, (tm, POOL_GROUP), 0) + tpos0
    out = []
    for gi, w in enumerate(POOL_WINDOWS):
        sl = slice(gi * POOL_GROUP, (gi + 1) * POOL_GROUP)
        s = ext_ref[POOL_HALO:POOL_HALO + tm, sl]
        for dlt in range(1, w):
            s = s + ext_ref[POOL_HALO - dlt:POOL_HALO - dlt + tm, sl]
        cnt = jnp.minimum(t_idx + 1, w).astype(F32)
        mixed = s / cnt - u_ref[:, sl]
        y = _dot(mixed.astype(BF16), pw_ref[gi]) * ps_ref[:, sl]
        out.append(y.astype(BF16))
    return out


def _post_kernel(*refs, even, seq, tm, fc):
    if even:
        (attn_ref, u_ref, halo_ref, pw_ref, ps_ref, wo_ref, x_ref, g_ref, w1_ref, w3_ref, w2_ref,
         o_ref, acc_ref, ext_ref) = refs
        half = attn_ref.shape[1]
        y = _dot(attn_ref[...], wo_ref[0:half, :])
        for gi, pg in enumerate(_pooled(u_ref, halo_ref, pw_ref, ps_ref, ext_ref, seq, tm)):
            y = y + _dot(pg, wo_ref[half + gi * POOL_GROUP:half + (gi + 1) * POOL_GROUP, :])
    else:
        attn_ref, wo_ref, x_ref, g_ref, w1_ref, w3_ref, w2_ref, o_ref, acc_ref = refs
        y = _dot(attn_ref[...], wo_ref[...])
    x1 = x_ref[...] + y
    h = _rms_rows(x1, g_ref[...]).astype(BF16)
    acc_ref[...] = x1

    def ffn_chunk(c, carry):
        cols = pl.ds(pl.multiple_of(c * fc, fc), fc)
        a = _dot(h, w1_ref[:, cols])
        b = _dot(h, w3_ref[:, cols])
        gate = (a * (1.0 / (1.0 + jnp.exp(-a))) * b).astype(BF16)
        acc_ref[...] += _dot(gate, w2_ref[cols, :])
        return carry

    lax.fori_loop(0, w1_ref.shape[1] // fc, ffn_chunk, 0)
    o_ref[...] = acc_ref[...]


def _post(x, attn, w_out, g, w1, w3, w2, tm, fc, seq, pool=None):
    n, d = x.shape
    even = pool is not None
    row = lambda wd: pl.BlockSpec((tm, wd), lambda i: (i, 0))
    ffn_specs = [_const_spec(w_out.shape), row(d), _const_spec((1, d)),
                 _const_spec(w1.shape), _const_spec(w3.shape), _const_spec(w2.shape)]
    scratch = [pltpu.VMEM((tm, d), F32)]
    if even:
        u, pool_w, pool_scale = pool
        halo_blocks = tm // POOL_HALO
        halo = pl.BlockSpec((POOL_HALO, u.shape[1]), lambda i: (jnp.maximum(i * halo_blocks - 1, 0), 0))
        in_specs = [row(attn.shape[1]), row(u.shape[1]), halo, _const_spec(pool_w.shape),
                    _const_spec(pool_scale.shape)] + ffn_specs
        args = (attn, u, u, pool_w, pool_scale, w_out, x, g, w1, w3, w2)
        scratch.append(pltpu.VMEM((POOL_HALO + tm, u.shape[1]), F32))
    else:
        in_specs = [row(attn.shape[1])] + ffn_specs
        args = (attn, w_out, x, g, w1, w3, w2)
    kern = functools.partial(_post_kernel, even=even, seq=seq, tm=tm, fc=fc)
    return pl.pallas_call(
        kern,
        grid=(n // tm,),
        in_specs=in_specs,
        out_specs=row(d),
        out_shape=jax.ShapeDtypeStruct((n, d), F32),
        scratch_shapes=scratch,
        compiler_params=_params(("parallel",)),
        name="post_even" if even else "post_odd",
    )(*args)


def _even_in_weights(w_in, q_g, k_g):
    d = w_in.shape[0]
    a_w, kv_w, qi_w = A_HEADS * HEAD_DIM, A_KV_HEADS * HEAD_DIM, IDX_HEADS * IDX_DIM
    o_k, o_v, o_qi = a_w, a_w + kv_w, a_w + 2 * kv_w
    o_ki = o_qi + qi_w
    o_wi = o_ki + IDX_DIM
    o_u = o_wi + IDX_HEADS
    idx_scale = (IDX_HEADS * IDX_DIM) ** -0.5
    w_idx = w_in[:, o_wi:o_u] * idx_scale
    zeros = lambda wd: jnp.zeros((d, wd), w_in.dtype)
    gq = q_g * (HEAD_DIM ** -0.5)
    gz = jnp.zeros_like(gq)
    t_cols, qg = [], []
    for hh in range(A_HEADS):
        wq = w_in[:, hh * HEAD_DIM:(hh + 1) * HEAD_DIM]
        first = hh // A_REP == 0
        t_cols += [wq, zeros(HEAD_DIM)] if first else [zeros(HEAD_DIM), wq]
        qg += [gq, gz] if first else [gz, gq]
    t_cols += [w_in[:, o_v:o_qi]]
    for hh in range(IDX_HEADS):
        t_cols += [w_in[:, o_qi + hh * IDX_DIM:o_qi + (hh + 1) * IDX_DIM], zeros(LANES - IDX_DIM)]
    t_cols += [w_idx, zeros(WT_ROWS - IDX_HEADS)]
    wt = jnp.concatenate(t_cols, axis=1).T.astype(BF16)
    wc = jnp.concatenate([w_in[:, o_k:o_v], w_in[:, o_ki:o_wi], w_idx, zeros(LANES - IDX_DIM - IDX_HEADS),
                          w_in[:, o_u:]], axis=1).astype(BF16)
    qg_rows = jnp.broadcast_to(jnp.concatenate(qg)[:, None], (QP_W, LANES))
    return wc, wt, qg_rows, jnp.tile(k_g, A_KV_HEADS)[None, :]


def _odd_in_weights(w_qkv):
    width = w_qkv.shape[1] // 3
    wk = w_qkv[:, width:2 * width].astype(BF16)
    wt = jnp.concatenate([w_qkv[:, :width], w_qkv[:, 2 * width:]], axis=1).T.astype(BF16)
    return wk, wt


def _trunk(x, norm_mix_g, norm_ffn_g, ev_w_in, ev_w_out, ev_q_norm_g, ev_k_norm_g, ev_pool_w,
           ev_pool_scale, od_w_qkv, od_w_out, ffn_w1, ffn_w3, ffn_w2, *, tm, fc):
    batch, seq, d = x.shape
    depth = norm_mix_g.shape[0]
    xf = x.reshape(batch * seq, d)
    for layer in range(depth):
        li = layer // 2
        g_mix = norm_mix_g[layer][None, :]
        g_ffn = norm_ffn_g[layer][None, :]
        w1, w3, w2 = (w[layer].astype(BF16) for w in (ffn_w1, ffn_w3, ffn_w2))
        if layer % 2 == 0:
            wc, wt, qg, kg = _even_in_weights(ev_w_in[li], ev_q_norm_g[li], ev_k_norm_g[li])
            qpT, k, vaugT, qipT, kiw, wrow, u = _even_proj(xf, g_mix, wc, wt, qg, kg, tm)
            bound = (HEAD_DIM ** 0.5) * jnp.max(jnp.abs(ev_q_norm_g[li])) * jnp.max(jnp.abs(ev_k_norm_g[li]))
            bound = (ROUNDING_SLACK * bound).astype(F32).reshape(1)
            attn = _dsa_attention(bound, qpT, k, vaugT, qipT, kiw, wrow, batch, seq)
            pool = (u, ev_pool_w[li].astype(BF16), ev_pool_scale[li][None, :])
            xf = _post(xf, attn, ev_w_out[li].astype(BF16), g_ffn, w1, w3, w2, tm, fc, seq, pool=pool)
        else:
            wk, wt = _odd_in_weights(od_w_qkv[li])
            qT, k, vT = _odd_proj(xf, g_mix, wk, wt, tm)
            attn = _sb_attention(qT, k, vT, batch, seq)
            xf = _post(xf, attn, od_w_out[li].astype(BF16), g_ffn, w1, w3, w2, tm, fc, seq)
    return xf.reshape(batch, seq, d)


def kernel(x, norm_mix_g, norm_ffn_g, ev_w_in, ev_w_out, ev_q_norm_g, ev_k_norm_g, ev_pool_w,
           ev_pool_scale, od_w_qkv, od_w_out, ffn_w1, ffn_w3, ffn_w2):
    return _trunk(x, norm_mix_g, norm_ffn_g, ev_w_in, ev_w_out, ev_q_norm_g, ev_k_norm_g, ev_pool_w,
                  ev_pool_scale, od_w_qkv, od_w_out, ffn_w1, ffn_w3, ffn_w2, tm=512, fc=256)
```

```python
import functools

import jax
import jax.numpy as jnp
import numpy as np
from jax import lax
from jax.experimental import pallas as pl
from jax.experimental.pallas import tpu as pltpu

F32 = jnp.float32
BF16 = jnp.bfloat16

HEAD_DIM = 64
A_HEADS = 8
A_KV_HEADS = 2
A_REP = A_HEADS // A_KV_HEADS
IDX_HEADS = 4
IDX_DIM = 64
TOPK_MAX = 256
POOL_WINDOWS = (2, 4, 8, 16)
POOL_GROUP = 128
POOL_WIDTH = POOL_GROUP * len(POOL_WINDOWS)
POOL_HALO = 16
EPS = 1e-6

LANES = 128
SUBLANES = 8
KEY_CHUNK = 256
INT_MIN = np.int32(-(2**31))
MASK_BIAS = -1e30
MAX_CONST_SHIFT = 20.0
ROUNDING_SLACK = 1.01
EXP_UNDERFLOW = 105.0
VMEM_LIMIT = 56 * 1024 * 1024

_NT = (((1,), (1,)), ((), ()))


def _dot(a, b):
    return jnp.dot(a, b, preferred_element_type=F32)


def _dot_nt(a, b):
    return lax.dot_general(a, b, _NT, preferred_element_type=F32)


def _split_bf16(x):
    hi = x.astype(BF16)
    lo = (x - hi.astype(F32)).astype(BF16)
    return hi, lo


def _rms_rows(x, g):
    ms = jnp.mean(x * x, axis=-1, keepdims=True)
    return x * lax.rsqrt(ms + EPS) * g


def _fold_rows(x, op):
    parts = [x[r:r + SUBLANES, :] for r in range(0, x.shape[0], SUBLANES)]
    while len(parts) > 1:
        nxt = [op(parts[a], parts[a + 1]) for a in range(0, len(parts) - 1, 2)]
        if len(parts) % 2:
            nxt.append(parts[-1])
        parts = nxt
    return parts[0]


def _const_spec(shape):
    nd = len(shape)
    return pl.BlockSpec(shape, lambda *_: (0,) * nd, pipeline_mode=pl.Buffered(1))


def _params(sem):
    return pltpu.CompilerParams(dimension_semantics=sem, vmem_limit_bytes=VMEM_LIMIT)


QB = KEY_CHUNK
QP_W = A_HEADS * LANES
QIP_W = IDX_HEADS * LANES
WT_ROWS = 16
_ROW_V = QP_W
_ROW_QI = _ROW_V + LANES
_ROW_W = _ROW_QI + QIP_W
EVEN_T_ROWS = _ROW_W + WT_ROWS
_COL_KIW = LANES
_COL_U = 2 * LANES
EVEN_COLS = _COL_U + POOL_WIDTH


def _even_proj_kernel(x_ref, g_ref, wc_ref, wt_ref, qg_ref, kg_ref, seg_ref,
                      qpT_ref, k_ref, vaugT_ref, qipT_ref, kiw_ref, wrow_ref, u_ref):
    h = _rms_rows(x_ref[...], g_ref[...]).astype(BF16)
    tm = h.shape[0]
    reps = tm // LANES
    blocks = tm // QB
    yT = _dot_nt(wt_ref[...], h)

    for hh in range(A_HEADS):
        rows = slice(hh * LANES, (hh + 1) * LANES)
        y = yT[rows, :]
        ss = jnp.sum(y * y, axis=0, keepdims=True)
        gain = jnp.concatenate([qg_ref[rows, :]] * reps, axis=1)
        qT = (y * lax.rsqrt(ss * (1.0 / HEAD_DIM) + EPS) * gain).astype(BF16)
        g, r = divmod(hh, A_REP)
        for j in range(blocks):
            qpT_ref[j, g, :, r * QB:(r + 1) * QB] = qT[:, j * QB:(j + 1) * QB]

    yc = _dot(h, wc_ref[...])
    y = yc[:, 0:LANES]
    hi, lo = _split_bf16(y * y)
    ss = _dot(hi, seg_ref[...]) + _dot(lo, seg_ref[...])
    k_ref[...] = (y * lax.rsqrt(ss * (1.0 / HEAD_DIM) + EPS) * kg_ref[...]).astype(BF16)

    vT = yT[_ROW_V:_ROW_V + LANES, :]
    one = jnp.ones((HEAD_DIM, tm), F32)
    aug = jnp.concatenate([vT[0:HEAD_DIM], one, one, vT[HEAD_DIM:LANES]], axis=0).astype(BF16)
    for j in range(tm // KEY_CHUNK):
        vaugT_ref[j] = aug[:, j * KEY_CHUNK:(j + 1) * KEY_CHUNK]

    for hh in range(IDX_HEADS):
        qiT = yT[_ROW_QI + hh * LANES:_ROW_QI + (hh + 1) * LANES, :].astype(BF16)
        for j in range(blocks):
            qipT_ref[j, :, hh * QB:(hh + 1) * QB] = qiT[:, j * QB:(j + 1) * QB]
            wrow_ref[j, :, hh * QB:(hh + 1) * QB] = jnp.broadcast_to(
                yT[_ROW_W + hh:_ROW_W + hh + 1, j * QB:(j + 1) * QB], (SUBLANES, QB))
    kiw_ref[...] = yc[:, _COL_KIW:_COL_KIW + LANES].astype(BF16)
    u_ref[...] = yc[:, _COL_U:_COL_U + POOL_WIDTH]


def _even_proj(x, g, wc, wt, qg, kg, tm):
    n, d = x.shape
    r = np.arange(LANES) // HEAD_DIM
    seg = jnp.asarray(r[:, None] == r[None, :], BF16)
    row = lambda wd: pl.BlockSpec((tm, wd), lambda i: (i, 0))
    cpt = tm // KEY_CHUNK
    bpt = tm // QB
    out_specs = [pl.BlockSpec((bpt, A_KV_HEADS, LANES, A_REP * QB), lambda i: (i, 0, 0, 0)), row(LANES),
                 pl.BlockSpec((cpt, 2 * LANES, KEY_CHUNK), lambda i: (i, 0, 0)),
                 pl.BlockSpec((bpt, LANES, IDX_HEADS * QB), lambda i: (i, 0, 0)), row(LANES),
                 pl.BlockSpec((bpt, SUBLANES, IDX_HEADS * QB), lambda i: (i, 0, 0)), row(POOL_WIDTH)]
    out_shape = [jax.ShapeDtypeStruct((n // QB, A_KV_HEADS, LANES, A_REP * QB), BF16),
                 jax.ShapeDtypeStruct((n, LANES), BF16),
                 jax.ShapeDtypeStruct((n // KEY_CHUNK, 2 * LANES, KEY_CHUNK), BF16),
                 jax.ShapeDtypeStruct((n // QB, LANES, IDX_HEADS * QB), BF16),
                 jax.ShapeDtypeStruct((n, LANES), BF16),
                 jax.ShapeDtypeStruct((n // QB, SUBLANES, IDX_HEADS * QB), F32),
                 jax.ShapeDtypeStruct((n, POOL_WIDTH), F32)]
    return pl.pallas_call(
        _even_proj_kernel,
        grid=(n // tm,),
        in_specs=[row(d), _const_spec((1, d)), _const_spec(wc.shape), _const_spec(wt.shape),
                  _const_spec(qg.shape), _const_spec((1, LANES)), _const_spec((LANES, LANES))],
        out_specs=out_specs,
        out_shape=out_shape,
        compiler_params=_params(("parallel",)),
        name="even_proj",
    )(x, g, wc, wt, qg, kg, seg)


def _dsa_kernel(shift_ref, qpT_ref, qipT_ref, wrow_ref, k_ref, kiw_ref, vaugT_ref, lower_ref, o_ref,
                keys_ref, hi_ref, lo_ref, bias_ref, acc_ref, *, n_sel):
    i = pl.program_id(1)
    nk = i + 1
    kpos = lax.broadcasted_iota(jnp.int32, (KEY_CHUNK, QB), 0)
    qpos = lax.broadcasted_iota(jnp.int32, (KEY_CHUNK, QB), 1)
    causal = kpos <= qpos
    logit_bound = shift_ref[0]
    small_bound = logit_bound <= MAX_CONST_SHIFT
    sel_bias = jnp.where(small_bound, -logit_bound, 0.0)

    def chunk_rows(c):
        return pl.ds(pl.multiple_of(c * KEY_CHUNK, KEY_CHUNK), KEY_CHUNK)

    def per_head(x, n):
        return [x[:, r * QB:(r + 1) * QB] for r in range(n)]

    def score_chunk(c, diagonal):
        d = _dot(kiw_ref[chunk_rows(c), :], qipT_ref[0])
        terms = per_head(jnp.maximum(d, 0.0) * wrow_ref[0, 0:1, :], IDX_HEADS)
        s = (terms[0] + terms[1]) + (terms[2] + terms[3])
        bits = lax.bitcast_convert_type(s, jnp.int32)
        key = bits ^ ((bits >> 31) & jnp.int32(0x7FFFFFFF))
        if diagonal:
            key = jnp.where(causal, key, INT_MIN)
        keys_ref[c] = key
        half = pl.ds(pl.multiple_of((c & 1) * KEY_CHUNK, KEY_CHUNK), KEY_CHUNK)
        hi_ref[c >> 1, half, :] = (key >> 16).astype(jnp.int16)
        lo_ref[c >> 1, half, :] = ((key & 0xFFFF) - 32768).astype(jnp.int16)

    def score_body(c, carry):
        score_chunk(c, False)
        return carry

    lax.fori_loop(0, i, score_body, 0)
    score_chunk(i, True)

    npair = (nk + 1) >> 1

    @pl.when((nk & 1) == 1)
    def _():
        pad = jnp.full((KEY_CHUNK, QB), -32768, jnp.int16)
        hi_ref[nk >> 1, KEY_CHUNK:2 * KEY_CHUNK, :] = pad
        lo_ref[nk >> 1, KEY_CHUNK:2 * KEY_CHUNK, :] = pad

    def count16(ref, pred):
        def body(c, acc):
            m = jnp.where(pred(ref[c]), jnp.int16(1), jnp.int16(0))
            parts = [m[r:r + 2 * SUBLANES, :] for r in range(0, 2 * KEY_CHUNK, 2 * SUBLANES)]
            while len(parts) > 1:
                parts = [parts[a] + parts[a + 1] for a in range(0, len(parts), 2)]
            return acc + parts[0]
        acc = lax.fori_loop(0, npair, body, jnp.zeros((2 * SUBLANES, QB), jnp.int16))
        return jnp.sum(acc.astype(jnp.int32), axis=0, keepdims=True)

    def bisect16(ref, target):
        def step(it, thr):
            bit = jnp.where(it == 0, jnp.int32(-32768), lax.shift_left(jnp.int32(1), 15 - it))
            cand = thr ^ bit
            cand16 = cand.astype(jnp.int16)
            cnt = count16(ref, lambda kc: kc >= cand16)
            return jnp.where(cnt >= target, cand, thr)
        return lax.fori_loop(0, 16, step, jnp.full((1, QB), -32768, jnp.int32))

    thr_hi = bisect16(hi_ref, n_sel)
    thr_hi16 = thr_hi.astype(jnp.int16)
    need_lo = n_sel - count16(hi_ref, lambda kc: kc > thr_hi16)

    def bucket_body(c, carry):
        lo_ref[c] = jnp.where(hi_ref[c] == thr_hi16, lo_ref[c], jnp.int16(-32768))
        return carry

    lax.fori_loop(0, npair, bucket_body, 0)
    thr_lo = bisect16(lo_ref, need_lo)
    thr = lax.shift_left(thr_hi, 16) | (thr_lo + 32768)
    thr = jnp.maximum(thr, INT_MIN + 1)

    def count(pred):
        def body(c, acc):
            return acc + _fold_rows(jnp.where(pred(keys_ref[c], c), 1, 0), jnp.add)
        acc = lax.fori_loop(0, nk, body, jnp.zeros((SUBLANES, QB), jnp.int32))
        return jnp.sum(acc, axis=0, keepdims=True)

    cnt_ge = count(lambda kc, c: kc >= thr)
    overflow = jnp.max(jnp.where(cnt_ge > n_sel, 1, 0))

    @pl.when(overflow == 0)
    def _():
        def body(c, carry):
            bias_ref[c] = jnp.where(keys_ref[c] >= thr, sel_bias, MASK_BIAS)
            return carry
        lax.fori_loop(0, nk, body, 0)

    @pl.when(overflow != 0)
    def _():
        need = (n_sel - count(lambda kc, c: kc > thr)).astype(F32)

        def body(c, earlier):
            kc = keys_ref[c]
            tie = kc == thr
            ones = jnp.where(tie, 1.0, 0.0)
            before = _dot(lower_ref[...], ones.astype(BF16)) + earlier
            tie_bias = jnp.where(before < need, sel_bias, MASK_BIAS)
            bias_ref[c] = jnp.where(kc > thr, sel_bias, jnp.where(tie, tie_bias, MASK_BIAS))
            return earlier + jnp.sum(_fold_rows(ones, jnp.add), axis=0, keepdims=True)
        lax.fori_loop(0, nk, body, jnp.zeros((1, QB), F32))

    acc_ref[...] = jnp.zeros_like(acc_ref)

    def group_logits(c, g):
        bias = bias_ref[c]
        return _dot(k_ref[chunk_rows(c), :], qpT_ref[0, g]) + jnp.concatenate([bias] * A_REP, axis=1)

    def attend(shifts):
        def attn_chunk(c, carry):
            xs = [group_logits(c, g) for g in range(A_KV_HEADS)]
            for g in range(A_KV_HEADS):
                x = xs[g]
                if shifts is not None:
                    x = x - shifts[g]
                p = jnp.exp(x).astype(BF16)
                acc_ref[g] += _dot(vaugT_ref[c, g * LANES:(g + 1) * LANES, :], p)
            return carry
        lax.fori_loop(0, nk, attn_chunk, 0)

    @pl.when(small_bound)
    def _():
        attend(None)

    @pl.when(jnp.logical_not(small_bound))
    def _():
        def max_chunk(c, maxes):
            return tuple(jnp.maximum(maxes[g], _fold_rows(group_logits(c, g), jnp.maximum))
                         for g in range(A_KV_HEADS))
        start = (jnp.full((SUBLANES, A_REP * QB), MASK_BIAS, F32),) * A_KV_HEADS
        maxes = lax.fori_loop(0, nk, max_chunk, start)
        attend([jnp.max(m, axis=0, keepdims=True) for m in maxes])

    for pr in range(A_HEADS // 2):
        parts = []
        for hh in (2 * pr, 2 * pr + 1):
            g, r = divmod(hh, A_REP)
            a = acc_ref[g, :, r * QB:(r + 1) * QB]
            lo, hi = a[0:HEAD_DIM], a[HEAD_DIM:LANES]
            parts.append(lo / hi if g == 0 else hi / lo)
        o_ref[:, pr * LANES:(pr + 1) * LANES] = jnp.concatenate(parts, axis=0).T.astype(BF16)


def _dsa_attention(logit_bound, qpT, k, vaugT, qipT, kiw, wrow, batch, seq):
    n = k.shape[0]
    nq = seq // QB
    nc = seq // KEY_CHUNK
    n_sel = min(TOPK_MAX, seq // 4)
    r = np.arange(KEY_CHUNK)
    lower = jnp.asarray(r[None, :] < r[:, None], BF16)
    full = lambda wd: pl.BlockSpec((seq, wd), lambda b, i: (b, 0))
    kern = functools.partial(_dsa_kernel, n_sel=n_sel)
    return pl.pallas_call(
        kern,
        grid=(batch, nq),
        in_specs=[pl.BlockSpec(memory_space=pltpu.SMEM),
                  pl.BlockSpec((1, A_KV_HEADS, LANES, A_REP * QB), lambda b, i: (b * nq + i, 0, 0, 0)),
                  pl.BlockSpec((1, LANES, IDX_HEADS * QB), lambda b, i: (b * nq + i, 0, 0)),
                  pl.BlockSpec((1, SUBLANES, IDX_HEADS * QB), lambda b, i: (b * nq + i, 0, 0)),
                  full(LANES), full(LANES), pl.BlockSpec((nc, 2 * LANES, KEY_CHUNK), lambda b, i: (b, 0, 0)),
                  _const_spec(lower.shape)],
        out_specs=pl.BlockSpec((QB, A_HEADS * HEAD_DIM), lambda b, i: (b * nq + i, 0)),
        out_shape=jax.ShapeDtypeStruct((n, A_HEADS * HEAD_DIM), BF16),
        scratch_shapes=[pltpu.VMEM((nc, KEY_CHUNK, QB), jnp.int32),
                        pltpu.VMEM((nc // 2, 2 * KEY_CHUNK, QB), jnp.int16),
                        pltpu.VMEM((nc // 2, 2 * KEY_CHUNK, QB), jnp.int16), pltpu.VMEM((nc, KEY_CHUNK, QB), F32),
                        pltpu.VMEM((A_KV_HEADS, LANES, A_REP * QB), F32)],
        compiler_params=_params(("parallel", "arbitrary")),
        name="dsa_attention",
    )(logit_bound, qpT, qipT, wrow, k, kiw, vaugT, lower)


def _odd_proj_kernel(x_ref, g_ref, wk_ref, wt_ref, qT_ref, k_ref, vT_ref):
    h = _rms_rows(x_ref[...], g_ref[...]).astype(BF16)
    tm = h.shape[0]
    width = wk_ref.shape[1]
    top_rows = lax.broadcasted_iota(jnp.int32, (LANES, tm), 0) < HEAD_DIM
    k_ref[...] = _dot(h, wk_ref[...]).astype(BF16)
    yT = _dot_nt(wt_ref[...], h)
    for pp in range(width // LANES):
        y = yT[pp * LANES:(pp + 1) * LANES, :] * (HEAD_DIM ** -0.5)
        first = jnp.where(top_rows, y, 0.0).astype(BF16)
        second = jnp.where(top_rows, 0.0, y).astype(BF16)
        for j in range(tm // QB):
            qT_ref[j, pp, :, 0:QB] = first[:, j * QB:(j + 1) * QB]
            qT_ref[j, pp, :, QB:2 * QB] = second[:, j * QB:(j + 1) * QB]
    vT = yT[width:2 * width, :].astype(BF16)
    for j in range(tm // KEY_CHUNK):
        vT_ref[j] = vT[:, j * KEY_CHUNK:(j + 1) * KEY_CHUNK]


def _odd_proj(x, g, wk, wt, tm):
    n, d = x.shape
    width = wk.shape[1]
    pairs = width // LANES
    return pl.pallas_call(
        _odd_proj_kernel,
        grid=(n // tm,),
        in_specs=[pl.BlockSpec((tm, d), lambda i: (i, 0)), _const_spec((1, d)), _const_spec(wk.shape),
                  _const_spec(wt.shape)],
        out_specs=[pl.BlockSpec((tm // QB, pairs, LANES, 2 * QB), lambda i: (i, 0, 0, 0)),
                   pl.BlockSpec((tm, width), lambda i: (i, 0)),
                   pl.BlockSpec((tm // KEY_CHUNK, width, KEY_CHUNK), lambda i: (i, 0, 0))],
        out_shape=[jax.ShapeDtypeStruct((n // QB, pairs, LANES, 2 * QB), BF16),
                   jax.ShapeDtypeStruct((n, width), BF16),
                   jax.ShapeDtypeStruct((n // KEY_CHUNK, width, KEY_CHUNK), BF16)],
        compiler_params=_params(("parallel",)),
        name="odd_proj",
    )(x, g, wk, wt)


def _sb_kernel(qT_ref, k_ref, vT_ref, tri_ref, o_ref, acc_ref, carry_ref):
    i = pl.program_id(2)
    q = qT_ref[0, 0]

    krel = lax.broadcasted_iota(jnp.int32, (KEY_CHUNK, 2 * QB), 0)
    qrel = lax.broadcasted_iota(jnp.int32, (KEY_CHUNK, 2 * QB), 1) & (QB - 1)
    strict = krel < qrel

    def masked(x, diagonal):
        if not diagonal:
            return x
        rows = x.shape[0] - KEY_CHUNK
        last = jnp.where(strict, x[rows:], 0.0)
        return last if rows == 0 else jnp.concatenate([x[:rows], last], axis=0)

    def block(first_chunk, chunks, carry, diagonal):
        keys = chunks * KEY_CHUNK
        start = pl.multiple_of(first_chunk * KEY_CHUNK, KEY_CHUNK)
        z = _dot(k_ref[pl.ds(start, keys), :], q)
        l1p = jnp.log(1.0 + jnp.exp(-jnp.abs(z)))
        sp = jnp.maximum(z, 0.0) + l1p
        ls = z - sp
        sp = masked(sp, diagonal)
        later = _dot(tri_ref[0:keys, 0:keys], sp.astype(BF16))
        a = masked(jnp.exp(ls - later - carry), diagonal).astype(BF16)
        pv = _dot(vT_ref[first_chunk], a[0:KEY_CHUNK])
        for cc in range(1, chunks):
            pv = pv + _dot(vT_ref[first_chunk + cc], a[cc * KEY_CHUNK:(cc + 1) * KEY_CHUNK])
        acc_ref[...] += pv
        return carry + jnp.sum(_fold_rows(sp, jnp.add), axis=0, keepdims=True)

    def unfinished(carry):
        return (jnp.min(carry) <= EXP_UNDERFLOW).astype(jnp.int32)

    acc_ref[...] = jnp.zeros_like(acc_ref)
    zero = jnp.zeros((1, 2 * QB), F32)

    @pl.when(i == 0)
    def _():
        block(0, 1, zero, True)

    @pl.when(i > 0)
    def _():
        carry_ref[...] = block(i - 1, 2, zero, True)

    def cond(st):
        return jnp.logical_and(st[0] >= 0, st[1] > 0)

    def body(st):
        carry = block(st[0], 1, st[2], False)
        return st[0] - 1, unfinished(carry), carry

    carry = carry_ref[...]
    lax.while_loop(cond, body, (i - 2, unfinished(carry), carry))
    oT = jnp.concatenate([acc_ref[0:HEAD_DIM, 0:QB], acc_ref[HEAD_DIM:LANES, QB:2 * QB]], axis=0)
    o_ref[...] = oT.T.astype(BF16)


def _sb_attention(qT, k, vT, batch, seq):
    n, width = k.shape
    pairs = width // LANES
    nq = seq // QB
    nc = seq // KEY_CHUNK
    r = np.arange(2 * KEY_CHUNK)
    tri = jnp.asarray(r[None, :] > r[:, None], BF16)
    return pl.pallas_call(
        _sb_kernel,
        grid=(batch, pairs, nq),
        in_specs=[pl.BlockSpec((1, 1, LANES, 2 * QB), lambda b, p, i: (b * nq + i, p, 0, 0)),
                  pl.BlockSpec((seq, LANES), lambda b, p, i: (b, p)),
                  pl.BlockSpec((nc, LANES, KEY_CHUNK), lambda b, p, i: (b, p, 0)),
                  _const_spec(tri.shape)],
        out_specs=pl.BlockSpec((QB, LANES), lambda b, p, i: (b * nq + i, p)),
        out_shape=jax.ShapeDtypeStruct((n, width), BF16),
        scratch_shapes=[pltpu.VMEM((LANES, 2 * QB), F32), pltpu.VMEM((1, 2 * QB), F32)],
        compiler_params=_params(("parallel", "parallel", "arbitrary")),
        name="sb_attention",
    )(qT, k, vT, tri)


def _pooled(u_ref, halo_ref, pw_ref, ps_ref, ext_ref, seq, tm):
    tpos0 = (pl.program_id(0) * tm) % seq
    halo = jnp.where(tpos0 == 0, 0.0, halo_ref[...])
    ext_ref[0:POOL_HALO, :] = halo
    ext_ref[POOL_HALO:POOL_HALO + tm, :] = u_ref[...]
    t_idx = lax.broadcasted_iota(jnp.int32, (tm, POOL_GROUP), 0) + tpos0
    out = []
    for gi, w in enumerate(POOL_WINDOWS):
        sl = slice(gi * POOL_GROUP, (gi + 1) * POOL_GROUP)
        s = ext_ref[POOL_HALO:POOL_HALO + tm, sl]
        for dlt in range(1, w):
            s = s + ext_ref[POOL_HALO - dlt:POOL_HALO - dlt + tm, sl]
        cnt = jnp.minimum(t_idx + 1, w).astype(F32)
        mixed = s / cnt - u_ref[:, sl]
        y = _dot(mixed.astype(BF16), pw_ref[gi]) * ps_ref[:, sl]
        out.append(y.astype(BF16))
    return out


def _post_kernel(*refs, even, seq, tm, fc):
    if even:
        (attn_ref, u_ref, halo_ref, pw_ref, ps_ref, wo_ref, x_ref, g_ref, w1_ref, w3_ref, w2_ref,
         o_ref, acc_ref, ext_ref) = refs
        half = attn_ref.shape[1]
        y = _dot(attn_ref[...], wo_ref[0:half, :])
        for gi, pg in enumerate(_pooled(u_ref, halo_ref, pw_ref, ps_ref, ext_ref, seq, tm)):
            y = y + _dot(pg, wo_ref[half + gi * POOL_GROUP:half + (gi + 1) * POOL_GROUP, :])
    else:
        attn_ref, wo_ref, x_ref, g_ref, w1_ref, w3_ref, w2_ref, o_ref, acc_ref = refs
        y = _dot(attn_ref[...], wo_ref[...])
    x1 = x_ref[...] + y
    h = _rms_rows(x1, g_ref[...]).astype(BF16)
    acc_ref[...] = x1

    def ffn_chunk(c, carry):
        cols = pl.ds(pl.multiple_of(c * fc, fc), fc)
        a = _dot(h, w1_ref[:, cols])
        b = _dot(h, w3_ref[:, cols])
        gate = (a * (1.0 / (1.0 + jnp.exp(-a))) * b).astype(BF16)
        acc_ref[...] += _dot(gate, w2_ref[cols, :])
        return carry

    lax.fori_loop(0, w1_ref.shape[1] // fc, ffn_chunk, 0)
    o_ref[...] = acc_ref[...]


def _post(x, attn, w_out, g, w1, w3, w2, tm, fc, seq, pool=None):
    n, d = x.shape
    even = pool is not None
    row = lambda wd: pl.BlockSpec((tm, wd), lambda i: (i, 0))
    ffn_specs = [_const_spec(w_out.shape), row(d), _const_spec((1, d)),
                 _const_spec(w1.shape), _const_spec(w3.shape), _const_spec(w2.shape)]
    scratch = [pltpu.VMEM((tm, d), F32)]
    if even:
        u, pool_w, pool_scale = pool
        halo_blocks = tm // POOL_HALO
        halo = pl.BlockSpec((POOL_HALO, u.shape[1]), lambda i: (jnp.maximum(i * halo_blocks - 1, 0), 0))
        in_specs = [row(attn.shape[1]), row(u.shape[1]), halo, _const_spec(pool_w.shape),
                    _const_spec(pool_scale.shape)] + ffn_specs
        args = (attn, u, u, pool_w, pool_scale, w_out, x, g, w1, w3, w2)
        scratch.append(pltpu.VMEM((POOL_HALO + tm, u.shape[1]), F32))
    else:
        in_specs = [row(attn.shape[1])] + ffn_specs
        args = (attn, w_out, x, g, w1, w3, w2)
    kern = functools.partial(_post_kernel, even=even, seq=seq, tm=tm, fc=fc)
    return pl.pallas_call(
        kern,
        grid=(n // tm,),
        in_specs=in_specs,
        out_specs=row(d),
        out_shape=jax.ShapeDtypeStruct((n, d), F32),
        scratch_shapes=scratch,
        compiler_params=_params(("parallel",)),
        name="post_even" if even else "post_odd",
    )(*args)


def _even_in_weights(w_in, q_g, k_g):
    d = w_in.shape[0]
    a_w, kv_w, qi_w = A_HEADS * HEAD_DIM, A_KV_HEADS * HEAD_DIM, IDX_HEADS * IDX_DIM
    o_k, o_v, o_qi = a_w, a_w + kv_w, a_w + 2 * kv_w
    o_ki = o_qi + qi_w
    o_wi = o_ki + IDX_DIM
    o_u = o_wi + IDX_HEADS
    idx_scale = (IDX_HEADS * IDX_DIM) ** -0.5
    w_idx = w_in[:, o_wi:o_u] * idx_scale
    zeros = lambda wd: jnp.zeros((d, wd), w_in.dtype)
    gq = q_g * (HEAD_DIM ** -0.5)
    gz = jnp.zeros_like(gq)
    t_cols, qg = [], []
    for hh in range(A_HEADS):
        wq = w_in[:, hh * HEAD_DIM:(hh + 1) * HEAD_DIM]
        first = hh // A_REP == 0
        t_cols += [wq, zeros(HEAD_DIM)] if first else [zeros(HEAD_DIM), wq]
        qg += [gq, gz] if first else [gz, gq]
    t_cols += [w_in[:, o_v:o_qi]]
    for hh in range(IDX_HEADS):
        t_cols += [w_in[:, o_qi + hh * IDX_DIM:o_qi + (hh + 1) * IDX_DIM], zeros(LANES - IDX_DIM)]
    t_cols += [w_idx, zeros(WT_ROWS - IDX_HEADS)]
    wt = jnp.concatenate(t_cols, axis=1).T.astype(BF16)
    wc = jnp.concatenate([w_in[:, o_k:o_v], w_in[:, o_ki:o_wi], w_idx, zeros(LANES - IDX_DIM - IDX_HEADS),
                          w_in[:, o_u:]], axis=1).astype(BF16)
    qg_rows = jnp.broadcast_to(jnp.concatenate(qg)[:, None], (QP_W, LANES))
    return wc, wt, qg_rows, jnp.tile(k_g, A_KV_HEADS)[None, :]


def _odd_in_weights(w_qkv):
    width = w_qkv.shape[1] // 3
    wk = w_qkv[:, width:2 * width].astype(BF16)
    wt = jnp.concatenate([w_qkv[:, :width], w_qkv[:, 2 * width:]], axis=1).T.astype(BF16)
    return wk, wt


def _trunk(x, norm_mix_g, norm_ffn_g, ev_w_in, ev_w_out, ev_q_norm_g, ev_k_norm_g, ev_pool_w,
           ev_pool_scale, od_w_qkv, od_w_out, ffn_w1, ffn_w3, ffn_w2, *, tm, fc):
    batch, seq, d = x.shape
    depth = norm_mix_g.shape[0]
    xf = x.reshape(batch * seq, d)
    for layer in range(depth):
        li = layer // 2
        g_mix = norm_mix_g[layer][None, :]
        g_ffn = norm_ffn_g[layer][None, :]
        w1, w3, w2 = (w[layer].astype(BF16) for w in (ffn_w1, ffn_w3, ffn_w2))
        if layer % 2 == 0:
            wc, wt, qg, kg = _even_in_weights(ev_w_in[li], ev_q_norm_g[li], ev_k_norm_g[li])
            qpT, k, vaugT, qipT, kiw, wrow, u = _even_proj(xf, g_mix, wc, wt, qg, kg, tm)
            bound = (HEAD_DIM ** 0.5) * jnp.max(jnp.abs(ev_q_norm_g[li])) * jnp.max(jnp.abs(ev_k_norm_g[li]))
            bound = (ROUNDING_SLACK * bound).astype(F32).reshape(1)
            attn = _dsa_attention(bound, qpT, k, vaugT, qipT, kiw, wrow, batch, seq)
            pool = (u, ev_pool_w[li].astype(BF16), ev_pool_scale[li][None, :])
            xf = _post(xf, attn, ev_w_out[li].astype(BF16), g_ffn, w1, w3, w2, tm, fc, seq, pool=pool)
        else:
            wk, wt = _odd_in_weights(od_w_qkv[li])
            qT, k, vT = _odd_proj(xf, g_mix, wk, wt, tm)
            attn = _sb_attention(qT, k, vT, batch, seq)
            xf = _post(xf, attn, od_w_out[li].astype(BF16), g_ffn, w1, w3, w2, tm, fc, seq)
    return xf.reshape(batch, seq, d)


def kernel(x, norm_mix_g, norm_ffn_g, ev_w_in, ev_w_out, ev_q_norm_g, ev_k_norm_g, ev_pool_w,
           ev_pool_scale, od_w_qkv, od_w_out, ffn_w1, ffn_w3, ffn_w2):
    return _trunk(x, norm_mix_g, norm_ffn_g, ev_w_in, ev_w_out, ev_q_norm_g, ev_k_norm_g, ev_pool_w,
                  ev_pool_scale, od_w_qkv, od_w_out, ffn_w1, ffn_w3, ffn_w2, tm=512, fc=1408)
```

```python
import functools

import jax
import jax.numpy as jnp
import numpy as np
from jax import lax
from jax.experimental import pallas as pl
from jax.experimental.pallas import tpu as pltpu

F32 = jnp.float32
BF16 = jnp.bfloat16

HEAD_DIM = 64
A_HEADS = 8
A_KV_HEADS = 2
A_REP = A_HEADS // A_KV_HEADS
IDX_HEADS = 4
IDX_DIM = 64
TOPK_MAX = 256
POOL_WINDOWS = (2, 4, 8, 16)
POOL_GROUP = 128
POOL_WIDTH = POOL_GROUP * len(POOL_WINDOWS)
POOL_HALO = 16
EPS = 1e-6

LANES = 128
SUBLANES = 8
KEY_CHUNK = 256
INT_MIN = np.int32(-(2**31))
MASK_BIAS = -1e30
MAX_CONST_SHIFT = 20.0
ROUNDING_SLACK = 1.01
EXP_UNDERFLOW = 105.0
VMEM_LIMIT = 56 * 1024 * 1024

_NT = (((1,), (1,)), ((), ()))


def _dot(a, b):
    return jnp.dot(a, b, preferred_element_type=F32)


def _dot_nt(a, b):
    return lax.dot_general(a, b, _NT, preferred_element_type=F32)


def _split_bf16(x):
    hi = x.astype(BF16)
    lo = (x - hi.astype(F32)).astype(BF16)
    return hi, lo


def _rms_rows(x, g):
    ms = jnp.mean(x * x, axis=-1, keepdims=True)
    return x * lax.rsqrt(ms + EPS) * g


def _fold_rows(x, op):
    parts = [x[r:r + SUBLANES, :] for r in range(0, x.shape[0], SUBLANES)]
    while len(parts) > 1:
        nxt = [op(parts[a], parts[a + 1]) for a in range(0, len(parts) - 1, 2)]
        if len(parts) % 2:
            nxt.append(parts[-1])
        parts = nxt
    return parts[0]


def _const_spec(shape):
    nd = len(shape)
    return pl.BlockSpec(shape, lambda *_: (0,) * nd, pipeline_mode=pl.Buffered(1))


def _params(sem):
    return pltpu.CompilerParams(dimension_semantics=sem, vmem_limit_bytes=VMEM_LIMIT)


QB = KEY_CHUNK
QP_W = A_HEADS * LANES
QIP_W = IDX_HEADS * LANES
WT_ROWS = 16
_ROW_V = QP_W
_ROW_QI = _ROW_V + LANES
_ROW_W = _ROW_QI + QIP_W
EVEN_T_ROWS = _ROW_W + WT_ROWS
_COL_KIW = LANES
_COL_U = 2 * LANES
EVEN_COLS = _COL_U + POOL_WIDTH


def _even_proj_kernel(x_ref, g_ref, wc_ref, wt_ref, qg_ref, kg_ref, seg_ref,
                      qpT_ref, k_ref, vaugT_ref, qipT_ref, kiw_ref, wrow_ref, u_ref):
    h = _rms_rows(x_ref[...], g_ref[...]).astype(BF16)
    tm = h.shape[0]
    reps = tm // LANES
    blocks = tm // QB
    yT = _dot_nt(wt_ref[...], h)

    for hh in range(A_HEADS):
        rows = slice(hh * LANES, (hh + 1) * LANES)
        y = yT[rows, :]
        ss = jnp.sum(y * y, axis=0, keepdims=True)
        gain = jnp.concatenate([qg_ref[rows, :]] * reps, axis=1)
        qT = (y * lax.rsqrt(ss * (1.0 / HEAD_DIM) + EPS) * gain).astype(BF16)
        g, r = divmod(hh, A_REP)
        for j in range(blocks):
            qpT_ref[j, g, :, r * QB:(r + 1) * QB] = qT[:, j * QB:(j + 1) * QB]

    yc = _dot(h, wc_ref[...])
    y = yc[:, 0:LANES]
    hi, lo = _split_bf16(y * y)
    ss = _dot(hi, seg_ref[...]) + _dot(lo, seg_ref[...])
    k_ref[...] = (y * lax.rsqrt(ss * (1.0 / HEAD_DIM) + EPS) * kg_ref[...]).astype(BF16)

    vT = yT[_ROW_V:_ROW_V + LANES, :]
    one = jnp.ones((HEAD_DIM, tm), F32)
    aug = jnp.concatenate([vT[0:HEAD_DIM], one, one, vT[HEAD_DIM:LANES]], axis=0).astype(BF16)
    for j in range(tm // KEY_CHUNK):
        vaugT_ref[j] = aug[:, j * KEY_CHUNK:(j + 1) * KEY_CHUNK]

    for hh in range(IDX_HEADS):
        qiT = yT[_ROW_QI + hh * LANES:_ROW_QI + (hh + 1) * LANES, :].astype(BF16)
        for j in range(blocks):
            qipT_ref[j, :, hh * QB:(hh + 1) * QB] = qiT[:, j * QB:(j + 1) * QB]
            wrow_ref[j, :, hh * QB:(hh + 1) * QB] = jnp.broadcast_to(
                yT[_ROW_W + hh:_ROW_W + hh + 1, j * QB:(j + 1) * QB], (SUBLANES, QB))
    kiw_ref[...] = yc[:, _COL_KIW:_COL_KIW + LANES].astype(BF16)
    u_ref[...] = yc[:, _COL_U:_COL_U + POOL_WIDTH]


def _even_proj(x, g, wc, wt, qg, kg, tm):
    n, d = x.shape
    r = np.arange(LANES) // HEAD_DIM
    seg = jnp.asarray(r[:, None] == r[None, :], BF16)
    row = lambda wd: pl.BlockSpec((tm, wd), lambda i: (i, 0))
    cpt = tm // KEY_CHUNK
    bpt = tm // QB
    out_specs = [pl.BlockSpec((bpt, A_KV_HEADS, LANES, A_REP * QB), lambda i: (i, 0, 0, 0)), row(LANES),
                 pl.BlockSpec((cpt, 2 * LANES, KEY_CHUNK), lambda i: (i, 0, 0)),
                 pl.BlockSpec((bpt, LANES, IDX_HEADS * QB), lambda i: (i, 0, 0)), row(LANES),
                 pl.BlockSpec((bpt, SUBLANES, IDX_HEADS * QB), lambda i: (i, 0, 0)), row(POOL_WIDTH)]
    out_shape = [jax.ShapeDtypeStruct((n // QB, A_KV_HEADS, LANES, A_REP * QB), BF16),
                 jax.ShapeDtypeStruct((n, LANES), BF16),
                 jax.ShapeDtypeStruct((n // KEY_CHUNK, 2 * LANES, KEY_CHUNK), BF16),
                 jax.ShapeDtypeStruct((n // QB, LANES, IDX_HEADS * QB), BF16),
                 jax.ShapeDtypeStruct((n, LANES), BF16),
                 jax.ShapeDtypeStruct((n // QB, SUBLANES, IDX_HEADS * QB), F32),
                 jax.ShapeDtypeStruct((n, POOL_WIDTH), F32)]
    return pl.pallas_call(
        _even_proj_kernel,
        grid=(n // tm,),
        in_specs=[row(d), _const_spec((1, d)), _const_spec(wc.shape), _const_spec(wt.shape),
                  _const_spec(qg.shape), _const_spec((1, LANES)), _const_spec((LANES, LANES))],
        out_specs=out_specs,
        out_shape=out_shape,
        compiler_params=_params(("parallel",)),
        name="even_proj",
    )(x, g, wc, wt, qg, kg, seg)


def _dsa_kernel(shift_ref, qpT_ref, qipT_ref, wrow_ref, k_ref, kiw_ref, vaugT_ref, lower_ref, o_ref,
                keys_ref, hi_ref, lo_ref, bias_ref, acc_ref, *, n_sel):
    i = pl.program_id(1)
    nk = i + 1
    kpos = lax.broadcasted_iota(jnp.int32, (KEY_CHUNK, QB), 0)
    qpos = lax.broadcasted_iota(jnp.int32, (KEY_CHUNK, QB), 1)
    causal = kpos <= qpos
    logit_bound = shift_ref[0]
    small_bound = logit_bound <= MAX_CONST_SHIFT
    sel_bias = jnp.where(small_bound, -logit_bound, 0.0)

    def chunk_rows(c):
        return pl.ds(pl.multiple_of(c * KEY_CHUNK, KEY_CHUNK), KEY_CHUNK)

    def per_head(x, n):
        return [x[:, r * QB:(r + 1) * QB] for r in range(n)]

    def score_matmul(c):
        return _dot(kiw_ref[chunk_rows(c), :], qipT_ref[0])

    def score_chunk(c, diagonal, d=None):
        d = score_matmul(c) if d is None else d
        terms = per_head(jnp.maximum(d, 0.0) * wrow_ref[0, 0:1, :], IDX_HEADS)
        s = (terms[0] + terms[1]) + (terms[2] + terms[3])
        bits = lax.bitcast_convert_type(s, jnp.int32)
        key = bits ^ ((bits >> 31) & jnp.int32(0x7FFFFFFF))
        if diagonal:
            key = jnp.where(causal, key, INT_MIN)
        keys_ref[c] = key
        half = pl.ds(pl.multiple_of((c & 1) * KEY_CHUNK, KEY_CHUNK), KEY_CHUNK)
        hi_ref[c >> 1, half, :] = (key >> 16).astype(jnp.int16)
        lo_ref[c >> 1, half, :] = ((key & 0xFFFF) - 32768).astype(jnp.int16)

    def score_pair(p, carry):
        ds = [score_matmul(2 * p), score_matmul(2 * p + 1)]
        score_chunk(2 * p, False, ds[0])
        score_chunk(2 * p + 1, False, ds[1])
        return carry

    lax.fori_loop(0, i >> 1, score_pair, 0)

    @pl.when((i & 1) == 1)
    def _():
        score_chunk(i - 1, False)

    score_chunk(i, True)

    npair = (nk + 1) >> 1

    @pl.when((nk & 1) == 1)
    def _():
        pad = jnp.full((KEY_CHUNK, QB), -32768, jnp.int16)
        hi_ref[nk >> 1, KEY_CHUNK:2 * KEY_CHUNK, :] = pad
        lo_ref[nk >> 1, KEY_CHUNK:2 * KEY_CHUNK, :] = pad

    def count16(ref, pred):
        def body(c, acc):
            m = jnp.where(pred(ref[c]), jnp.int16(1), jnp.int16(0))
            parts = [m[r:r + 2 * SUBLANES, :] for r in range(0, 2 * KEY_CHUNK, 2 * SUBLANES)]
            while len(parts) > 1:
                parts = [parts[a] + parts[a + 1] for a in range(0, len(parts), 2)]
            return acc + parts[0]
        acc = lax.fori_loop(0, npair, body, jnp.zeros((2 * SUBLANES, QB), jnp.int16))
        return jnp.sum(acc.astype(jnp.int32), axis=0, keepdims=True)

    def bisect16(ref, target):
        def step(it, thr):
            bit = jnp.where(it == 0, jnp.int32(-32768), lax.shift_left(jnp.int32(1), 15 - it))
            cand = thr ^ bit
            cand16 = cand.astype(jnp.int16)
            cnt = count16(ref, lambda kc: kc >= cand16)
            return jnp.where(cnt >= target, cand, thr)
        return lax.fori_loop(0, 16, step, jnp.full((1, QB), -32768, jnp.int32))

    thr_hi = bisect16(hi_ref, n_sel)
    thr_hi16 = thr_hi.astype(jnp.int16)
    need_lo = n_sel - count16(hi_ref, lambda kc: kc > thr_hi16)

    def bucket_body(c, carry):
        lo_ref[c] = jnp.where(hi_ref[c] == thr_hi16, lo_ref[c], jnp.int16(-32768))
        return carry

    lax.fori_loop(0, npair, bucket_body, 0)
    thr_lo = bisect16(lo_ref, need_lo)
    thr = lax.shift_left(thr_hi, 16) | (thr_lo + 32768)
    thr = jnp.maximum(thr, INT_MIN + 1)

    def count(pred):
        def body(c, acc):
            return acc + _fold_rows(jnp.where(pred(keys_ref[c], c), 1, 0), jnp.add)
        acc = lax.fori_loop(0, nk, body, jnp.zeros((SUBLANES, QB), jnp.int32))
        return jnp.sum(acc, axis=0, keepdims=True)

    cnt_ge = count(lambda kc, c: kc >= thr)
    overflow = jnp.max(jnp.where(cnt_ge > n_sel, 1, 0))

    @pl.when(overflow == 0)
    def _():
        def body(c, carry):
            bias_ref[c] = jnp.where(keys_ref[c] >= thr, sel_bias, MASK_BIAS)
            return carry
        lax.fori_loop(0, nk, body, 0)

    @pl.when(overflow != 0)
    def _():
        need = (n_sel - count(lambda kc, c: kc > thr)).astype(F32)

        def body(c, earlier):
            kc = keys_ref[c]
            tie = kc == thr
            ones = jnp.where(tie, 1.0, 0.0)
            before = _dot(lower_ref[...], ones.astype(BF16)) + earlier
            tie_bias = jnp.where(before < need, sel_bias, MASK_BIAS)
            bias_ref[c] = jnp.where(kc > thr, sel_bias, jnp.where(tie, tie_bias, MASK_BIAS))
            return earlier + jnp.sum(_fold_rows(ones, jnp.add), axis=0, keepdims=True)
        lax.fori_loop(0, nk, body, jnp.zeros((1, QB), F32))

    acc_ref[...] = jnp.zeros_like(acc_ref)

    def group_logits(c, g):
        bias = bias_ref[c]
        return _dot(k_ref[chunk_rows(c), :], qpT_ref[0, g]) + jnp.concatenate([bias] * A_REP, axis=1)

    def attend(shifts):
        def attn_chunks(chunks):
            xs = [[group_logits(c, g) for g in range(A_KV_HEADS)] for c in chunks]
            for c, xc in zip(chunks, xs):
                for g in range(A_KV_HEADS):
                    x = xc[g]
                    if shifts is not None:
                        x = x - shifts[g]
                    p = jnp.exp(x).astype(BF16)
                    acc_ref[g] += _dot(vaugT_ref[c, g * LANES:(g + 1) * LANES, :], p)

        def attn_pair(p, carry):
            attn_chunks([2 * p, 2 * p + 1])
            return carry
        lax.fori_loop(0, nk >> 1, attn_pair, 0)

        @pl.when((nk & 1) == 1)
        def _():
            attn_chunks([nk - 1])

    @pl.when(small_bound)
    def _():
        attend(None)

    @pl.when(jnp.logical_not(small_bound))
    def _():
        def max_chunk(c, maxes):
            return tuple(jnp.maximum(maxes[g], _fold_rows(group_logits(c, g), jnp.maximum))
                         for g in range(A_KV_HEADS))
        start = (jnp.full((SUBLANES, A_REP * QB), MASK_BIAS, F32),) * A_KV_HEADS
        maxes = lax.fori_loop(0, nk, max_chunk, start)
        attend([jnp.max(m, axis=0, keepdims=True) for m in maxes])

    for pr in range(A_HEADS // 2):
        parts = []
        for hh in (2 * pr, 2 * pr + 1):
            g, r = divmod(hh, A_REP)
            a = acc_ref[g, :, r * QB:(r + 1) * QB]
            lo, hi = a[0:HEAD_DIM], a[HEAD_DIM:LANES]
            parts.append(lo / hi if g == 0 else hi / lo)
        o_ref[:, pr * LANES:(pr + 1) * LANES] = jnp.concatenate(parts, axis=0).T.astype(BF16)


def _dsa_attention(logit_bound, qpT, k, vaugT, qipT, kiw, wrow, batch, seq):
    n = k.shape[0]
    nq = seq // QB
    nc = seq // KEY_CHUNK
    n_sel = min(TOPK_MAX, seq // 4)
    r = np.arange(KEY_CHUNK)
    lower = jnp.asarray(r[None, :] < r[:, None], BF16)
    full = lambda wd: pl.BlockSpec((seq, wd), lambda b, i: (b, 0))
    kern = functools.partial(_dsa_kernel, n_sel=n_sel)
    return pl.pallas_call(
        kern,
        grid=(batch, nq),
        in_specs=[pl.BlockSpec(memory_space=pltpu.SMEM),
                  pl.BlockSpec((1, A_KV_HEADS, LANES, A_REP * QB), lambda b, i: (b * nq + i, 0, 0, 0)),
                  pl.BlockSpec((1, LANES, IDX_HEADS * QB), lambda b, i: (b * nq + i, 0, 0)),
                  pl.BlockSpec((1, SUBLANES, IDX_HEADS * QB), lambda b, i: (b * nq + i, 0, 0)),
                  full(LANES), full(LANES), pl.BlockSpec((nc, 2 * LANES, KEY_CHUNK), lambda b, i: (b, 0, 0)),
                  _const_spec(lower.shape)],
        out_specs=pl.BlockSpec((QB, A_HEADS * HEAD_DIM), lambda b, i: (b * nq + i, 0)),
        out_shape=jax.ShapeDtypeStruct((n, A_HEADS * HEAD_DIM), BF16),
        scratch_shapes=[pltpu.VMEM((nc, KEY_CHUNK, QB), jnp.int32),
                        pltpu.VMEM((nc // 2, 2 * KEY_CHUNK, QB), jnp.int16),
                        pltpu.VMEM((nc // 2, 2 * KEY_CHUNK, QB), jnp.int16), pltpu.VMEM((nc, KEY_CHUNK, QB), F32),
                        pltpu.VMEM((A_KV_HEADS, LANES, A_REP * QB), F32)],
        compiler_params=_params(("parallel", "arbitrary")),
        name="dsa_attention",
    )(logit_bound, qpT, qipT, wrow, k, kiw, vaugT, lower)


def _odd_proj_kernel(x_ref, g_ref, wk_ref, wt_ref, qT_ref, k_ref, vT_ref):
    h = _rms_rows(x_ref[...], g_ref[...]).astype(BF16)
    tm = h.shape[0]
    width = wk_ref.shape[1]
    top_rows = lax.broadcasted_iota(jnp.int32, (LANES, tm), 0) < HEAD_DIM
    k_ref[...] = _dot(h, wk_ref[...]).astype(BF16)
    yT = _dot_nt(wt_ref[...], h)
    for pp in range(width // LANES):
        y = yT[pp * LANES:(pp + 1) * LANES, :] * (HEAD_DIM ** -0.5)
        first = jnp.where(top_rows, y, 0.0).astype(BF16)
        second = jnp.where(top_rows, 0.0, y).astype(BF16)
        for j in range(tm // QB):
            qT_ref[j, pp, :, 0:QB] = first[:, j * QB:(j + 1) * QB]
            qT_ref[j, pp, :, QB:2 * QB] = second[:, j * QB:(j + 1) * QB]
    vT = yT[width:2 * width, :].astype(BF16)
    for j in range(tm // KEY_CHUNK):
        vT_ref[j] = vT[:, j * KEY_CHUNK:(j + 1) * KEY_CHUNK]


def _odd_proj(x, g, wk, wt, tm):
    n, d = x.shape
    width = wk.shape[1]
    pairs = width // LANES
    return pl.pallas_call(
        _odd_proj_kernel,
        grid=(n // tm,),
        in_specs=[pl.BlockSpec((tm, d), lambda i: (i, 0)), _const_spec((1, d)), _const_spec(wk.shape),
                  _const_spec(wt.shape)],
        out_specs=[pl.BlockSpec((tm // QB, pairs, LANES, 2 * QB), lambda i: (i, 0, 0, 0)),
                   pl.BlockSpec((tm, width), lambda i: (i, 0)),
                   pl.BlockSpec((tm // KEY_CHUNK, width, KEY_CHUNK), lambda i: (i, 0, 0))],
        out_shape=[jax.ShapeDtypeStruct((n // QB, pairs, LANES, 2 * QB), BF16),
                   jax.ShapeDtypeStruct((n, width), BF16),
                   jax.ShapeDtypeStruct((n // KEY_CHUNK, width, KEY_CHUNK), BF16)],
        compiler_params=_params(("parallel",)),
        name="odd_proj",
    )(x, g, wk, wt)


def _sb_kernel(qT_ref, k_ref, vT_ref, tri_ref, o_ref, acc_ref, carry_ref):
    i = pl.program_id(2)
    q = qT_ref[0, 0]

    krel = lax.broadcasted_iota(jnp.int32, (KEY_CHUNK, 2 * QB), 0)
    qrel = lax.broadcasted_iota(jnp.int32, (KEY_CHUNK, 2 * QB), 1) & (QB - 1)
    strict = krel < qrel

    def masked(x, diagonal):
        if not diagonal:
            return x
        rows = x.shape[0] - KEY_CHUNK
        last = jnp.where(strict, x[rows:], 0.0)
        return last if rows == 0 else jnp.concatenate([x[:rows], last], axis=0)

    def block(first_chunk, chunks, carry, diagonal):
        keys = chunks * KEY_CHUNK
        start = pl.multiple_of(first_chunk * KEY_CHUNK, KEY_CHUNK)
        z = _dot(k_ref[pl.ds(start, keys), :], q)
        l1p = jnp.log(1.0 + jnp.exp(-jnp.abs(z)))
        sp = jnp.maximum(z, 0.0) + l1p
        ls = z - sp
        sp = masked(sp, diagonal)
        sums = [jnp.sum(_fold_rows(sp[cc * KEY_CHUNK:(cc + 1) * KEY_CHUNK], jnp.add), axis=0, keepdims=True)
                for cc in range(chunks)]
        pv = None
        for cc in range(chunks):
            rows = slice(cc * KEY_CHUNK, (cc + 1) * KEY_CHUNK)
            later = _dot(tri_ref[...], sp[rows].astype(BF16)) + carry
            for nxt in range(cc + 1, chunks):
                later = later + sums[nxt]
            a = jnp.exp(ls[rows] - later)
            if diagonal and cc == chunks - 1:
                a = jnp.where(strict, a, 0.0)
            term = _dot(vT_ref[first_chunk + cc], a.astype(BF16))
            pv = term if pv is None else pv + term
        acc_ref[...] += pv
        for cc in range(chunks):
            carry = carry + sums[cc]
        return carry

    def unfinished(carry):
        return (jnp.min(carry) <= EXP_UNDERFLOW).astype(jnp.int32)

    acc_ref[...] = jnp.zeros_like(acc_ref)
    zero = jnp.zeros((1, 2 * QB), F32)

    @pl.when(i == 0)
    def _():
        block(0, 1, zero, True)

    @pl.when(i > 0)
    def _():
        carry_ref[...] = block(i - 1, 2, zero, True)

    def cond(st):
        return jnp.logical_and(st[0] >= 0, st[1] > 0)

    def body(st):
        carry = block(st[0], 1, st[2], False)
        return st[0] - 1, unfinished(carry), carry

    carry = carry_ref[...]
    lax.while_loop(cond, body, (i - 2, unfinished(carry), carry))
    oT = jnp.concatenate([acc_ref[0:HEAD_DIM, 0:QB], acc_ref[HEAD_DIM:LANES, QB:2 * QB]], axis=0)
    o_ref[...] = oT.T.astype(BF16)


def _sb_attention(qT, k, vT, batch, seq):
    n, width = k.shape
    pairs = width // LANES
    nq = seq // QB
    nc = seq // KEY_CHUNK
    r = np.arange(KEY_CHUNK)
    tri = jnp.asarray(r[None, :] > r[:, None], BF16)
    return pl.pallas_call(
        _sb_kernel,
        grid=(batch, pairs, nq),
        in_specs=[pl.BlockSpec((1, 1, LANES, 2 * QB), lambda b, p, i: (b * nq + i, p, 0, 0)),
                  pl.BlockSpec((seq, LANES), lambda b, p, i: (b, p)),
                  pl.BlockSpec((nc, LANES, KEY_CHUNK), lambda b, p, i: (b, p, 0)),
                  _const_spec(tri.shape)],
        out_specs=pl.BlockSpec((QB, LANES), lambda b, p, i: (b * nq + i, p)),
        out_shape=jax.ShapeDtypeStruct((n, width), BF16),
        scratch_shapes=[pltpu.VMEM((LANES, 2 * QB), F32), pltpu.VMEM((1, 2 * QB), F32)],
        compiler_params=_params(("parallel", "parallel", "arbitrary")),
        name="sb_attention",
    )(qT, k, vT, tri)


def _pooled(u_ref, halo_ref, pw_ref, ps_ref, ext_ref, seq, tm):
    tpos0 = (pl.program_id(0) * tm) % seq
    halo = jnp.where(tpos0 == 0, 0.0, halo_ref[...])
    ext_ref[0:POOL_HALO, :] = halo
    ext_ref[POOL_HALO:POOL_HALO + tm, :] = u_ref[...]
    t_idx = lax.broadcasted_iota(jnp.int32, (tm, POOL_GROUP), 0) + tpos0
    out = []
    for gi, w in enumerate(POOL_WINDOWS):
        sl = slice(gi * POOL_GROUP, (gi + 1) * POOL_GROUP)
        s = ext_ref[POOL_HALO:POOL_HALO + tm, sl]
        for dlt in range(1, w):
            s = s + ext_ref[POOL_HALO - dlt:POOL_HALO - dlt + tm, sl]
        cnt = jnp.minimum(t_idx + 1, w).astype(F32)
        mixed = s / cnt - u_ref[:, sl]
        y = _dot(mixed.astype(BF16), pw_ref[gi]) * ps_ref[:, sl]
        out.append(y.astype(BF16))
    return out


def _post_kernel(*refs, even, seq, tm, fc):
    if even:
        (attn_ref, u_ref, halo_ref, pw_ref, ps_ref, wo_ref, x_ref, g_ref, w1_ref, w3_ref, w2_ref,
         o_ref, acc_ref, ext_ref) = refs
        half = attn_ref.shape[1]
        y = _dot(attn_ref[...], wo_ref[0:half, :])
        for gi, pg in enumerate(_pooled(u_ref, halo_ref, pw_ref, ps_ref, ext_ref, seq, tm)):
            y = y + _dot(pg, wo_ref[half + gi * POOL_GROUP:half + (gi + 1) * POOL_GROUP, :])
    else:
        attn_ref, wo_ref, x_ref, g_ref, w1_ref, w3_ref, w2_ref, o_ref, acc_ref = refs
        y = _dot(attn_ref[...], wo_ref[...])
    x1 = x_ref[...] + y
    h = _rms_rows(x1, g_ref[...]).astype(BF16)
    acc_ref[...] = x1

    def ffn_chunk(c, carry):
        cols = pl.ds(pl.multiple_of(c * fc, fc), fc)
        a = _dot(h, w1_ref[:, cols])
        b = _dot(h, w3_ref[:, cols])
        gate = (a * (1.0 / (1.0 + jnp.exp(-a))) * b).astype(BF16)
        acc_ref[...] += _dot(gate, w2_ref[cols, :])
        return carry

    lax.fori_loop(0, w1_ref.shape[1] // fc, ffn_chunk, 0)
    o_ref[...] = acc_ref[...]


def _post(x, attn, w_out, g, w1, w3, w2, tm, fc, seq, pool=None):
    n, d = x.shape
    even = pool is not None
    row = lambda wd: pl.BlockSpec((tm, wd), lambda i: (i, 0))
    ffn_specs = [_const_spec(w_out.shape), row(d), _const_spec((1, d)),
                 _const_spec(w1.shape), _const_spec(w3.shape), _const_spec(w2.shape)]
    scratch = [pltpu.VMEM((tm, d), F32)]
    if even:
        u, pool_w, pool_scale = pool
        halo_blocks = tm // POOL_HALO
        halo = pl.BlockSpec((POOL_HALO, u.shape[1]), lambda i: (jnp.maximum(i * halo_blocks - 1, 0), 0))
        in_specs = [row(attn.shape[1]), row(u.shape[1]), halo, _const_spec(pool_w.shape),
                    _const_spec(pool_scale.shape)] + ffn_specs
        args = (attn, u, u, pool_w, pool_scale, w_out, x, g, w1, w3, w2)
        scratch.append(pltpu.VMEM((POOL_HALO + tm, u.shape[1]), F32))
    else:
        in_specs = [row(attn.shape[1])] + ffn_specs
        args = (attn, w_out, x, g, w1, w3, w2)
    kern = functools.partial(_post_kernel, even=even, seq=seq, tm=tm, fc=fc)
    return pl.pallas_call(
        kern,
        grid=(n // tm,),
        in_specs=in_specs,
        out_specs=row(d),
        out_shape=jax.ShapeDtypeStruct((n, d), F32),
        scratch_shapes=scratch,
        compiler_params=_params(("parallel",)),
        name="post_even" if even else "post_odd",
    )(*args)


def _even_in_weights(w_in, q_g, k_g):
    d = w_in.shape[0]
    a_w, kv_w, qi_w = A_HEADS * HEAD_DIM, A_KV_HEADS * HEAD_DIM, IDX_HEADS * IDX_DIM
    o_k, o_v, o_qi = a_w, a_w + kv_w, a_w + 2 * kv_w
    o_ki = o_qi + qi_w
    o_wi = o_ki + IDX_DIM
    o_u = o_wi + IDX_HEADS
    idx_scale = (IDX_HEADS * IDX_DIM) ** -0.5
    w_idx = w_in[:, o_wi:o_u] * idx_scale

    def head_pad(w, heads, before, after):
        w = jnp.pad(w.reshape(d, heads, HEAD_DIM), ((0, 0), (0, 0), (before, after)))
        return w.reshape(d, heads * LANES)

    wq = w_in[:, :a_w]
    t_cols = [head_pad(wq[:, :a_w // 2], A_REP, 0, HEAD_DIM), head_pad(wq[:, a_w // 2:], A_REP, HEAD_DIM, 0),
              w_in[:, o_v:o_qi], head_pad(w_in[:, o_qi:o_ki], IDX_HEADS, 0, LANES - IDX_DIM),
              jnp.pad(w_idx, ((0, 0), (0, WT_ROWS - IDX_HEADS)))]
    wt = jnp.concatenate(t_cols, axis=1).T.astype(BF16)
    kiw = jnp.pad(jnp.concatenate([w_in[:, o_ki:o_wi], w_idx], axis=1),
                  ((0, 0), (0, LANES - IDX_DIM - IDX_HEADS)))
    wc = jnp.concatenate([w_in[:, o_k:o_v], kiw, w_in[:, o_u:]], axis=1).astype(BF16)
    gq = jnp.pad((q_g * (HEAD_DIM ** -0.5))[None, :], ((0, 0), (0, HEAD_DIM)))
    qg = jnp.concatenate([jnp.tile(gq, (A_REP, 1)), jnp.tile(jnp.roll(gq, HEAD_DIM, axis=1), (A_REP, 1))])
    qg_rows = jnp.broadcast_to(qg.reshape(QP_W)[:, None], (QP_W, LANES))
    return wc, wt, qg_rows, jnp.tile(k_g, A_KV_HEADS)[None, :]


def _odd_in_weights(w_qkv):
    width = w_qkv.shape[1] // 3
    wk = w_qkv[:, width:2 * width].astype(BF16)
    wt = jnp.concatenate([w_qkv[:, :width], w_qkv[:, 2 * width:]], axis=1).T.astype(BF16)
    return wk, wt


def _trunk(x, norm_mix_g, norm_ffn_g, ev_w_in, ev_w_out, ev_q_norm_g, ev_k_norm_g, ev_pool_w,
           ev_pool_scale, od_w_qkv, od_w_out, ffn_w1, ffn_w3, ffn_w2, *, tm, fc):
    batch, seq, d = x.shape
    depth = norm_mix_g.shape[0]
    xf = x.reshape(batch * seq, d)
    for layer in range(depth):
        li = layer // 2
        g_mix = norm_mix_g[layer][None, :]
        g_ffn = norm_ffn_g[layer][None, :]
        w1, w3, w2 = (w[layer].astype(BF16) for w in (ffn_w1, ffn_w3, ffn_w2))
        if layer % 2 == 0:
            wc, wt, qg, kg = _even_in_weights(ev_w_in[li], ev_q_norm_g[li], ev_k_norm_g[li])
            qpT, k, vaugT, qipT, kiw, wrow, u = _even_proj(xf, g_mix, wc, wt, qg, kg, tm)
            bound = (HEAD_DIM ** 0.5) * jnp.max(jnp.abs(ev_q_norm_g[li])) * jnp.max(jnp.abs(ev_k_norm_g[li]))
            bound = (ROUNDING_SLACK * bound).astype(F32).reshape(1)
            attn = _dsa_attention(bound, qpT, k, vaugT, qipT, kiw, wrow, batch, seq)
            pool = (u, ev_pool_w[li].astype(BF16), ev_pool_scale[li][None, :])
            xf = _post(xf, attn, ev_w_out[li].astype(BF16), g_ffn, w1, w3, w2, tm, fc, seq, pool=pool)
        else:
            wk, wt = _odd_in_weights(od_w_qkv[li])
            qT, k, vT = _odd_proj(xf, g_mix, wk, wt, tm)
            attn = _sb_attention(qT, k, vT, batch, seq)
            xf = _post(xf, attn, od_w_out[li].astype(BF16), g_ffn, w1, w3, w2, tm, fc, seq)
    return xf.reshape(batch, seq, d)


def kernel(x, norm_mix_g, norm_ffn_g, ev_w_in, ev_w_out, ev_q_norm_g, ev_k_norm_g, ev_pool_w,
           ev_pool_scale, od_w_qkv, od_w_out, ffn_w1, ffn_w3, ffn_w2):
    return _trunk(x, norm_mix_g, norm_ffn_g, ev_w_in, ev_w_out, ev_q_norm_g, ev_k_norm_g, ev_pool_w,
                  ev_pool_scale, od_w_qkv, od_w_out, ffn_w1, ffn_w3, ffn_w2, tm=512, fc=1408)
```

```python
import functools

import jax
import jax.numpy as jnp
import numpy as np
from jax import lax
from jax.experimental import pallas as pl
from jax.experimental.pallas import tpu as pltpu

F32 = jnp.float32
BF16 = jnp.bfloat16

HEAD_DIM = 64
A_HEADS = 8
A_KV_HEADS = 2
A_REP = A_HEADS // A_KV_HEADS
IDX_HEADS = 4
IDX_DIM = 64
TOPK_MAX = 256
POOL_WINDOWS = (2, 4, 8, 16)
POOL_GROUP = 128
POOL_WIDTH = POOL_GROUP * len(POOL_WINDOWS)
POOL_HALO = 16
EPS = 1e-6

LANES = 128
SUBLANES = 8
KEY_CHUNK = 256
INT_MIN = np.int32(-(2**31))
MASK_BIAS = -1e30
MAX_CONST_SHIFT = 20.0
ROUNDING_SLACK = 1.01
EXP_UNDERFLOW = 105.0
SB_TAIL = 192
VMEM_LIMIT = 56 * 1024 * 1024

_NT = (((1,), (1,)), ((), ()))


def _dot(a, b):
    return jnp.dot(a, b, preferred_element_type=F32)


def _dot_nt(a, b):
    return lax.dot_general(a, b, _NT, preferred_element_type=F32)


def _split_bf16(x):
    hi = x.astype(BF16)
    lo = (x - hi.astype(F32)).astype(BF16)
    return hi, lo


def _rms_rows(x, g):
    ms = jnp.mean(x * x, axis=-1, keepdims=True)
    return x * lax.rsqrt(ms + EPS) * g


def _fold_rows(x, op):
    parts = [x[r:r + SUBLANES, :] for r in range(0, x.shape[0], SUBLANES)]
    while len(parts) > 1:
        nxt = [op(parts[a], parts[a + 1]) for a in range(0, len(parts) - 1, 2)]
        if len(parts) % 2:
            nxt.append(parts[-1])
        parts = nxt
    return parts[0]


def _const_spec(shape):
    nd = len(shape)
    return pl.BlockSpec(shape, lambda *_: (0,) * nd, pipeline_mode=pl.Buffered(1))


def _params(sem):
    return pltpu.CompilerParams(dimension_semantics=sem, vmem_limit_bytes=VMEM_LIMIT)


QB = KEY_CHUNK
QP_W = A_HEADS * LANES
QIP_W = IDX_HEADS * LANES
WT_ROWS = 16
_ROW_V = QP_W
_ROW_QI = _ROW_V + LANES
_ROW_W = _ROW_QI + QIP_W
EVEN_T_ROWS = _ROW_W + WT_ROWS
_COL_KIW = LANES
_COL_U = 2 * LANES
EVEN_COLS = _COL_U + POOL_WIDTH


def _even_proj_kernel(x_ref, g_ref, wc_ref, wt_ref, qg_ref, kg_ref, seg_ref,
                      qpT_ref, k_ref, vaugT_ref, qipT_ref, kiw_ref, wrow_ref, u_ref):
    h = _rms_rows(x_ref[...], g_ref[...]).astype(BF16)
    tm = h.shape[0]
    reps = tm // LANES
    blocks = tm // QB
    yT = _dot_nt(wt_ref[...], h)

    for hh in range(A_HEADS):
        rows = slice(hh * LANES, (hh + 1) * LANES)
        y = yT[rows, :]
        ss = jnp.sum(y * y, axis=0, keepdims=True)
        gain = jnp.concatenate([qg_ref[rows, :]] * reps, axis=1)
        qT = (y * lax.rsqrt(ss * (1.0 / HEAD_DIM) + EPS) * gain).astype(BF16)
        g, r = divmod(hh, A_REP)
        for j in range(blocks):
            qpT_ref[j, g, :, r * QB:(r + 1) * QB] = qT[:, j * QB:(j + 1) * QB]

    yc = _dot(h, wc_ref[...])
    y = yc[:, 0:LANES]
    hi, lo = _split_bf16(y * y)
    ss = _dot(hi, seg_ref[...]) + _dot(lo, seg_ref[...])
    k_ref[...] = (y * lax.rsqrt(ss * (1.0 / HEAD_DIM) + EPS) * kg_ref[...]).astype(BF16)

    vT = yT[_ROW_V:_ROW_V + LANES, :]
    one = jnp.ones((HEAD_DIM, tm), F32)
    aug = jnp.concatenate([vT[0:HEAD_DIM], one, one, vT[HEAD_DIM:LANES]], axis=0).astype(BF16)
    for j in range(tm // KEY_CHUNK):
        vaugT_ref[j] = aug[:, j * KEY_CHUNK:(j + 1) * KEY_CHUNK]

    for hh in range(IDX_HEADS):
        qiT = yT[_ROW_QI + hh * LANES:_ROW_QI + (hh + 1) * LANES, :].astype(BF16)
        for j in range(blocks):
            qipT_ref[j, :, hh * QB:(hh + 1) * QB] = qiT[:, j * QB:(j + 1) * QB]
            wrow_ref[j, :, hh * QB:(hh + 1) * QB] = jnp.broadcast_to(
                yT[_ROW_W + hh:_ROW_W + hh + 1, j * QB:(j + 1) * QB], (SUBLANES, QB))
    kiw_ref[...] = yc[:, _COL_KIW:_COL_KIW + LANES].astype(BF16)
    u_ref[...] = yc[:, _COL_U:_COL_U + POOL_WIDTH]


def _even_proj(x, g, wc, wt, qg, kg, tm):
    n, d = x.shape
    r = np.arange(LANES) // HEAD_DIM
    seg = jnp.asarray(r[:, None] == r[None, :], BF16)
    row = lambda wd: pl.BlockSpec((tm, wd), lambda i: (i, 0))
    cpt = tm // KEY_CHUNK
    bpt = tm // QB
    out_specs = [pl.BlockSpec((bpt, A_KV_HEADS, LANES, A_REP * QB), lambda i: (i, 0, 0, 0)), row(LANES),
                 pl.BlockSpec((cpt, 2 * LANES, KEY_CHUNK), lambda i: (i, 0, 0)),
                 pl.BlockSpec((bpt, LANES, IDX_HEADS * QB), lambda i: (i, 0, 0)), row(LANES),
                 pl.BlockSpec((bpt, SUBLANES, IDX_HEADS * QB), lambda i: (i, 0, 0)), row(POOL_WIDTH)]
    out_shape = [jax.ShapeDtypeStruct((n // QB, A_KV_HEADS, LANES, A_REP * QB), BF16),
                 jax.ShapeDtypeStruct((n, LANES), BF16),
                 jax.ShapeDtypeStruct((n // KEY_CHUNK, 2 * LANES, KEY_CHUNK), BF16),
                 jax.ShapeDtypeStruct((n // QB, LANES, IDX_HEADS * QB), BF16),
                 jax.ShapeDtypeStruct((n, LANES), BF16),
                 jax.ShapeDtypeStruct((n // QB, SUBLANES, IDX_HEADS * QB), F32),
                 jax.ShapeDtypeStruct((n, POOL_WIDTH), F32)]
    return pl.pallas_call(
        _even_proj_kernel,
        grid=(n // tm,),
        in_specs=[row(d), _const_spec((1, d)), _const_spec(wc.shape), _const_spec(wt.shape),
                  _const_spec(qg.shape), _const_spec((1, LANES)), _const_spec((LANES, LANES))],
        out_specs=out_specs,
        out_shape=out_shape,
        compiler_params=_params(("parallel",)),
        name="even_proj",
    )(x, g, wc, wt, qg, kg, seg)


def _dsa_kernel(shift_ref, qpT_ref, qipT_ref, wrow_ref, k_ref, kiw_ref, vaugT_ref, lower_ref, o_ref,
                keys_ref, hi_ref, lo_ref, bias_ref, acc_ref, *, n_sel):
    i = pl.program_id(1)
    nk = i + 1
    kpos = lax.broadcasted_iota(jnp.int32, (KEY_CHUNK, QB), 0)
    qpos = lax.broadcasted_iota(jnp.int32, (KEY_CHUNK, QB), 1)
    causal = kpos <= qpos
    logit_bound = shift_ref[0]
    small_bound = logit_bound <= MAX_CONST_SHIFT
    sel_bias = jnp.where(small_bound, -logit_bound, 0.0)

    def chunk_rows(c):
        return pl.ds(pl.multiple_of(c * KEY_CHUNK, KEY_CHUNK), KEY_CHUNK)

    def per_head(x, n):
        return [x[:, r * QB:(r + 1) * QB] for r in range(n)]

    def score_matmul(c):
        return _dot(kiw_ref[chunk_rows(c), :], qipT_ref[0])

    def score_chunk(c, diagonal, d=None):
        d = score_matmul(c) if d is None else d
        terms = per_head(jnp.maximum(d, 0.0) * wrow_ref[0, 0:1, :], IDX_HEADS)
        s = (terms[0] + terms[1]) + (terms[2] + terms[3])
        bits = lax.bitcast_convert_type(s, jnp.int32)
        key = bits ^ ((bits >> 31) & jnp.int32(0x7FFFFFFF))
        if diagonal:
            key = jnp.where(causal, key, INT_MIN)
        keys_ref[c] = key
        half = pl.ds(pl.multiple_of((c & 1) * KEY_CHUNK, KEY_CHUNK), KEY_CHUNK)
        hi_ref[c >> 1, half, :] = (key >> 16).astype(jnp.int16)
        lo_ref[c >> 1, half, :] = ((key & 0xFFFF) - 32768).astype(jnp.int16)

    def score_pair(p, carry):
        ds = [score_matmul(2 * p), score_matmul(2 * p + 1)]
        score_chunk(2 * p, False, ds[0])
        score_chunk(2 * p + 1, False, ds[1])
        return carry

    lax.fori_loop(0, i >> 1, score_pair, 0)

    @pl.when((i & 1) == 1)
    def _():
        score_chunk(i - 1, False)

    score_chunk(i, True)

    npair = (nk + 1) >> 1

    @pl.when((nk & 1) == 1)
    def _():
        pad = jnp.full((KEY_CHUNK, QB), -32768, jnp.int16)
        hi_ref[nk >> 1, KEY_CHUNK:2 * KEY_CHUNK, :] = pad
        lo_ref[nk >> 1, KEY_CHUNK:2 * KEY_CHUNK, :] = pad

    def count16(ref, pred):
        def body(c, acc):
            m = jnp.where(pred(ref[c]), jnp.int16(1), jnp.int16(0))
            parts = [m[r:r + 2 * SUBLANES, :] for r in range(0, 2 * KEY_CHUNK, 2 * SUBLANES)]
            while len(parts) > 1:
                parts = [parts[a] + parts[a + 1] for a in range(0, len(parts), 2)]
            return acc + parts[0]
        acc = lax.fori_loop(0, npair, body, jnp.zeros((2 * SUBLANES, QB), jnp.int16))
        return jnp.sum(acc.astype(jnp.int32), axis=0, keepdims=True)

    def bisect16(ref, target):
        def step(it, thr):
            bit = jnp.where(it == 0, jnp.int32(-32768), lax.shift_left(jnp.int32(1), 15 - it))
            cand = thr ^ bit
            cand16 = cand.astype(jnp.int16)
            cnt = count16(ref, lambda kc: kc >= cand16)
            return jnp.where(cnt >= target, cand, thr)
        return lax.fori_loop(0, 16, step, jnp.full((1, QB), -32768, jnp.int32))

    thr_hi = bisect16(hi_ref, n_sel)
    thr_hi16 = thr_hi.astype(jnp.int16)
    need_lo = n_sel - count16(hi_ref, lambda kc: kc > thr_hi16)

    def bucket_body(c, carry):
        lo_ref[c] = jnp.where(hi_ref[c] == thr_hi16, lo_ref[c], jnp.int16(-32768))
        return carry

    lax.fori_loop(0, npair, bucket_body, 0)
    thr_lo = bisect16(lo_ref, need_lo)
    thr = lax.shift_left(thr_hi, 16) | (thr_lo + 32768)
    thr = jnp.maximum(thr, INT_MIN + 1)

    def count(pred):
        def body(c, acc):
            return acc + _fold_rows(jnp.where(pred(keys_ref[c], c), 1, 0), jnp.add)
        acc = lax.fori_loop(0, nk, body, jnp.zeros((SUBLANES, QB), jnp.int32))
        return jnp.sum(acc, axis=0, keepdims=True)

    cnt_ge = count(lambda kc, c: kc >= thr)
    overflow = jnp.max(jnp.where(cnt_ge > n_sel, 1, 0))

    @pl.when(overflow == 0)
    def _():
        def body(c, carry):
            bias_ref[c] = jnp.where(keys_ref[c] >= thr, sel_bias, MASK_BIAS)
            return carry
        lax.fori_loop(0, nk, body, 0)

    @pl.when(overflow != 0)
    def _():
        need = (n_sel - count(lambda kc, c: kc > thr)).astype(F32)

        def body(c, earlier):
            kc = keys_ref[c]
            tie = kc == thr
            ones = jnp.where(tie, 1.0, 0.0)
            before = _dot(lower_ref[...], ones.astype(BF16)) + earlier
            tie_bias = jnp.where(before < need, sel_bias, MASK_BIAS)
            bias_ref[c] = jnp.where(kc > thr, sel_bias, jnp.where(tie, tie_bias, MASK_BIAS))
            return earlier + jnp.sum(_fold_rows(ones, jnp.add), axis=0, keepdims=True)
        lax.fori_loop(0, nk, body, jnp.zeros((1, QB), F32))

    acc_ref[...] = jnp.zeros_like(acc_ref)

    def group_logits(c, g):
        bias = bias_ref[c]
        return _dot(k_ref[chunk_rows(c), :], qpT_ref[0, g]) + jnp.concatenate([bias] * A_REP, axis=1)

    def attend(shifts):
        def attn_chunks(chunks):
            xs = [[group_logits(c, g) for g in range(A_KV_HEADS)] for c in chunks]
            for c, xc in zip(chunks, xs):
                for g in range(A_KV_HEADS):
                    x = xc[g]
                    if shifts is not None:
                        x = x - shifts[g]
                    p = jnp.exp(x).astype(BF16)
                    acc_ref[g] += _dot(vaugT_ref[c, g * LANES:(g + 1) * LANES, :], p)

        def attn_pair(p, carry):
            attn_chunks([2 * p, 2 * p + 1])
            return carry
        lax.fori_loop(0, nk >> 1, attn_pair, 0)

        @pl.when((nk & 1) == 1)
        def _():
            attn_chunks([nk - 1])

    @pl.when(small_bound)
    def _():
        attend(None)

    @pl.when(jnp.logical_not(small_bound))
    def _():
        def max_chunk(c, maxes):
            return tuple(jnp.maximum(maxes[g], _fold_rows(group_logits(c, g), jnp.maximum))
                         for g in range(A_KV_HEADS))
        start = (jnp.full((SUBLANES, A_REP * QB), MASK_BIAS, F32),) * A_KV_HEADS
        maxes = lax.fori_loop(0, nk, max_chunk, start)
        attend([jnp.max(m, axis=0, keepdims=True) for m in maxes])

    for pr in range(A_HEADS // 2):
        parts = []
        for hh in (2 * pr, 2 * pr + 1):
            g, r = divmod(hh, A_REP)
            a = acc_ref[g, :, r * QB:(r + 1) * QB]
            lo, hi = a[0:HEAD_DIM], a[HEAD_DIM:LANES]
            parts.append(lo / hi if g == 0 else hi / lo)
        o_ref[:, pr * LANES:(pr + 1) * LANES] = jnp.concatenate(parts, axis=0).T.astype(BF16)


def _dsa_attention(logit_bound, qpT, k, vaugT, qipT, kiw, wrow, batch, seq):
    n = k.shape[0]
    nq = seq // QB
    nc = seq // KEY_CHUNK
    n_sel = min(TOPK_MAX, seq // 4)
    r = np.arange(KEY_CHUNK)
    lower = jnp.asarray(r[None, :] < r[:, None], BF16)
    full = lambda wd: pl.BlockSpec((seq, wd), lambda b, i: (b, 0))
    kern = functools.partial(_dsa_kernel, n_sel=n_sel)
    return pl.pallas_call(
        kern,
        grid=(batch, nq),
        in_specs=[pl.BlockSpec(memory_space=pltpu.SMEM),
                  pl.BlockSpec((1, A_KV_HEADS, LANES, A_REP * QB), lambda b, i: (b * nq + i, 0, 0, 0)),
                  pl.BlockSpec((1, LANES, IDX_HEADS * QB), lambda b, i: (b * nq + i, 0, 0)),
                  pl.BlockSpec((1, SUBLANES, IDX_HEADS * QB), lambda b, i: (b * nq + i, 0, 0)),
                  full(LANES), full(LANES), pl.BlockSpec((nc, 2 * LANES, KEY_CHUNK), lambda b, i: (b, 0, 0)),
                  _const_spec(lower.shape)],
        out_specs=pl.BlockSpec((QB, A_HEADS * HEAD_DIM), lambda b, i: (b * nq + i, 0)),
        out_shape=jax.ShapeDtypeStruct((n, A_HEADS * HEAD_DIM), BF16),
        scratch_shapes=[pltpu.VMEM((nc, KEY_CHUNK, QB), jnp.int32),
                        pltpu.VMEM((nc // 2, 2 * KEY_CHUNK, QB), jnp.int16),
                        pltpu.VMEM((nc // 2, 2 * KEY_CHUNK, QB), jnp.int16), pltpu.VMEM((nc, KEY_CHUNK, QB), F32),
                        pltpu.VMEM((A_KV_HEADS, LANES, A_REP * QB), F32)],
        compiler_params=_params(("parallel", "arbitrary")),
        name="dsa_attention",
    )(logit_bound, qpT, qipT, wrow, k, kiw, vaugT, lower)


def _odd_proj_kernel(x_ref, g_ref, wk_ref, wt_ref, qT_ref, k_ref, vT_ref):
    h = _rms_rows(x_ref[...], g_ref[...]).astype(BF16)
    tm = h.shape[0]
    width = wk_ref.shape[1]
    top_rows = lax.broadcasted_iota(jnp.int32, (LANES, tm), 0) < HEAD_DIM
    k_ref[...] = _dot(h, wk_ref[...]).astype(BF16)
    yT = _dot_nt(wt_ref[...], h)
    for pp in range(width // LANES):
        y = yT[pp * LANES:(pp + 1) * LANES, :] * (HEAD_DIM ** -0.5)
        first = jnp.where(top_rows, y, 0.0).astype(BF16)
        second = jnp.where(top_rows, 0.0, y).astype(BF16)
        for j in range(tm // QB):
            qT_ref[j, pp, :, 0:QB] = first[:, j * QB:(j + 1) * QB]
            qT_ref[j, pp, :, QB:2 * QB] = second[:, j * QB:(j + 1) * QB]
    vT = yT[width:2 * width, :].astype(BF16)
    for j in range(tm // KEY_CHUNK):
        vT_ref[j] = vT[:, j * KEY_CHUNK:(j + 1) * KEY_CHUNK]


def _odd_proj(x, g, wk, wt, tm):
    n, d = x.shape
    width = wk.shape[1]
    pairs = width // LANES
    return pl.pallas_call(
        _odd_proj_kernel,
        grid=(n // tm,),
        in_specs=[pl.BlockSpec((tm, d), lambda i: (i, 0)), _const_spec((1, d)), _const_spec(wk.shape),
                  _const_spec(wt.shape)],
        out_specs=[pl.BlockSpec((tm // QB, pairs, LANES, 2 * QB), lambda i: (i, 0, 0, 0)),
                   pl.BlockSpec((tm, width), lambda i: (i, 0)),
                   pl.BlockSpec((tm // KEY_CHUNK, width, KEY_CHUNK), lambda i: (i, 0, 0))],
        out_shape=[jax.ShapeDtypeStruct((n // QB, pairs, LANES, 2 * QB), BF16),
                   jax.ShapeDtypeStruct((n, width), BF16),
                   jax.ShapeDtypeStruct((n // KEY_CHUNK, width, KEY_CHUNK), BF16)],
        compiler_params=_params(("parallel",)),
        name="odd_proj",
    )(x, g, wk, wt)


def _sb_kernel(qT_ref, k_ref, vT_ref, tri_ref, o_ref, acc_ref, carry_ref):
    i = pl.program_id(2)
    q = qT_ref[0, 0]

    krel = lax.broadcasted_iota(jnp.int32, (KEY_CHUNK, 2 * QB), 0)
    qrel = lax.broadcasted_iota(jnp.int32, (KEY_CHUNK, 2 * QB), 1) & (QB - 1)
    strict = krel < qrel

    def masked(x, diagonal):
        if not diagonal:
            return x
        rows = x.shape[0] - KEY_CHUNK
        last = jnp.where(strict, x[rows:], 0.0)
        return last if rows == 0 else jnp.concatenate([x[:rows], last], axis=0)

    def block(segs, carry, diagonal):
        chunk0, off0, _ = segs[0]
        start = pl.multiple_of(chunk0 * KEY_CHUNK, KEY_CHUNK) + off0
        bounds = [sum(r for _, _, r in segs[:n]) for n in range(len(segs) + 1)]
        z = _dot(k_ref[pl.ds(start, bounds[-1]), :], q)
        l1p = jnp.log(1.0 + jnp.exp(-jnp.abs(z)))
        sp = jnp.maximum(z, 0.0) + l1p
        ls = z - sp
        sp = masked(sp, diagonal)
        sums = [jnp.sum(_fold_rows(sp[bounds[n]:bounds[n + 1]], jnp.add), axis=0, keepdims=True)
                for n in range(len(segs))]
        pv = None
        for n, (chunk, off, rows) in enumerate(segs):
            later = _dot(tri_ref[0:rows, 0:rows], sp[bounds[n]:bounds[n + 1]].astype(BF16)) + carry
            for nxt in range(n + 1, len(segs)):
                later = later + sums[nxt]
            a = jnp.exp(ls[bounds[n]:bounds[n + 1]] - later)
            if diagonal and n == len(segs) - 1:
                a = jnp.where(strict, a, 0.0)
            parts = [a.astype(BF16)]
            if off:
                parts.insert(0, jnp.zeros((off, 2 * QB), BF16))
            if off + rows < KEY_CHUNK:
                parts.append(jnp.zeros((KEY_CHUNK - off - rows, 2 * QB), BF16))
            term = _dot(vT_ref[chunk], parts[0] if len(parts) == 1 else jnp.concatenate(parts, axis=0))
            pv = term if pv is None else pv + term
        acc_ref[...] += pv
        for n in range(len(segs)):
            carry = carry + sums[n]
        return carry

    def unfinished(carry):
        return (jnp.min(carry) <= EXP_UNDERFLOW).astype(jnp.int32)

    acc_ref[...] = jnp.zeros_like(acc_ref)
    zero = jnp.zeros((1, 2 * QB), F32)
    carry_ref[...] = zero

    @pl.when(i == 0)
    def _():
        block([(0, 0, KEY_CHUNK)], zero, True)

    @pl.when(i > 0)
    def _():
        carry = block([(i - 1, KEY_CHUNK - SB_TAIL, SB_TAIL), (i, 0, KEY_CHUNK)], zero, True)
        carry_ref[...] = carry

        @pl.when(unfinished(carry) > 0)
        def _():
            carry_ref[...] = block([(i - 1, 0, KEY_CHUNK - SB_TAIL)], carry, False)

    def cond(st):
        return jnp.logical_and(st[0] >= 0, st[1] > 0)

    def body(st):
        carry = block([(st[0], 0, KEY_CHUNK)], st[2], False)
        return st[0] - 1, unfinished(carry), carry

    carry = carry_ref[...]
    lax.while_loop(cond, body, (i - 2, unfinished(carry), carry))
    oT = jnp.concatenate([acc_ref[0:HEAD_DIM, 0:QB], acc_ref[HEAD_DIM:LANES, QB:2 * QB]], axis=0)
    o_ref[...] = oT.T.astype(BF16)


def _sb_attention(qT, k, vT, batch, seq):
    n, width = k.shape
    pairs = width // LANES
    nq = seq // QB
    nc = seq // KEY_CHUNK
    r = np.arange(KEY_CHUNK)
    tri = jnp.asarray(r[None, :] > r[:, None], BF16)
    return pl.pallas_call(
        _sb_kernel,
        grid=(batch, pairs, nq),
        in_specs=[pl.BlockSpec((1, 1, LANES, 2 * QB), lambda b, p, i: (b * nq + i, p, 0, 0)),
                  pl.BlockSpec((seq, LANES), lambda b, p, i: (b, p)),
                  pl.BlockSpec((nc, LANES, KEY_CHUNK), lambda b, p, i: (b, p, 0)),
                  _const_spec(tri.shape)],
        out_specs=pl.BlockSpec((QB, LANES), lambda b, p, i: (b * nq + i, p)),
        out_shape=jax.ShapeDtypeStruct((n, width), BF16),
        scratch_shapes=[pltpu.VMEM((LANES, 2 * QB), F32), pltpu.VMEM((1, 2 * QB), F32)],
        compiler_params=_params(("parallel", "parallel", "arbitrary")),
        name="sb_attention",
    )(qT, k, vT, tri)


def _pooled(u_ref, halo_ref, pw_ref, ps_ref, ext_ref, seq, tm):
    tpos0 = (pl.program_id(0) * tm) % seq
    halo = jnp.where(tpos0 == 0, 0.0, halo_ref[...])
    ext_ref[0:POOL_HALO, :] = halo
    ext_ref[POOL_HALO:POOL_HALO + tm, :] = u_ref[...]
    t_idx = lax.broadcasted_iota(jnp.int32, (tm, POOL_GROUP), 0) + tpos0
    out = []
    for gi, w in enumerate(POOL_WINDOWS):
        sl = slice(gi * POOL_GROUP, (gi + 1) * POOL_GROUP)
        s = ext_ref[POOL_HALO:POOL_HALO + tm, sl]
        for dlt in range(1, w):
            s = s + ext_ref[POOL_HALO - dlt:POOL_HALO - dlt + tm, sl]
        cnt = jnp.minimum(t_idx + 1, w).astype(F32)
        mixed = s / cnt - u_ref[:, sl]
        y = _dot(mixed.astype(BF16), pw_ref[gi]) * ps_ref[:, sl]
        out.append(y.astype(BF16))
    return out


def _post_kernel(*refs, even, seq, tm, fc):
    if even:
        (attn_ref, u_ref, halo_ref, pw_ref, ps_ref, wo_ref, x_ref, g_ref, w1_ref, w3_ref, w2_ref,
         o_ref, acc_ref, ext_ref) = refs
        half = attn_ref.shape[1]
        y = _dot(attn_ref[...], wo_ref[0:half, :])
        for gi, pg in enumerate(_pooled(u_ref, halo_ref, pw_ref, ps_ref, ext_ref, seq, tm)):
            y = y + _dot(pg, wo_ref[half + gi * POOL_GROUP:half + (gi + 1) * POOL_GROUP, :])
    else:
        attn_ref, wo_ref, x_ref, g_ref, w1_ref, w3_ref, w2_ref, o_ref, acc_ref = refs
        y = _dot(attn_ref[...], wo_ref[...])
    x1 = x_ref[...] + y
    h = _rms_rows(x1, g_ref[...]).astype(BF16)
    acc_ref[...] = x1

    def ffn_chunk(c, carry):
        cols = pl.ds(pl.multiple_of(c * fc, fc), fc)
        a = _dot(h, w1_ref[:, cols])
        b = _dot(h, w3_ref[:, cols])
        gate = (a * (1.0 / (1.0 + jnp.exp(-a))) * b).astype(BF16)
        acc_ref[...] += _dot(gate, w2_ref[cols, :])
        return carry

    lax.fori_loop(0, w1_ref.shape[1] // fc, ffn_chunk, 0)
    o_ref[...] = acc_ref[...]


def _post(x, attn, w_out, g, w1, w3, w2, tm, fc, seq, pool=None):
    n, d = x.shape
    even = pool is not None
    row = lambda wd: pl.BlockSpec((tm, wd), lambda i: (i, 0))
    ffn_specs = [_const_spec(w_out.shape), row(d), _const_spec((1, d)),
                 _const_spec(w1.shape), _const_spec(w3.shape), _const_spec(w2.shape)]
    scratch = [pltpu.VMEM((tm, d), F32)]
    if even:
        u, pool_w, pool_scale = pool
        halo_blocks = tm // POOL_HALO
        halo = pl.BlockSpec((POOL_HALO, u.shape[1]), lambda i: (jnp.maximum(i * halo_blocks - 1, 0), 0))
        in_specs = [row(attn.shape[1]), row(u.shape[1]), halo, _const_spec(pool_w.shape),
                    _const_spec(pool_scale.shape)] + ffn_specs
        args = (attn, u, u, pool_w, pool_scale, w_out, x, g, w1, w3, w2)
        scratch.append(pltpu.VMEM((POOL_HALO + tm, u.shape[1]), F32))
    else:
        in_specs = [row(attn.shape[1])] + ffn_specs
        args = (attn, w_out, x, g, w1, w3, w2)
    kern = functools.partial(_post_kernel, even=even, seq=seq, tm=tm, fc=fc)
    return pl.pallas_call(
        kern,
        grid=(n // tm,),
        in_specs=in_specs,
        out_specs=row(d),
        out_shape=jax.ShapeDtypeStruct((n, d), F32),
        scratch_shapes=scratch,
        compiler_params=_params(("parallel",)),
        name="post_even" if even else "post_odd",
    )(*args)


def _even_in_weights(w_in, q_g, k_g):
    d = w_in.shape[0]
    a_w, kv_w, qi_w = A_HEADS * HEAD_DIM, A_KV_HEADS * HEAD_DIM, IDX_HEADS * IDX_DIM
    o_k, o_v, o_qi = a_w, a_w + kv_w, a_w + 2 * kv_w
    o_ki = o_qi + qi_w
    o_wi = o_ki + IDX_DIM
    o_u = o_wi + IDX_HEADS
    idx_scale = (IDX_HEADS * IDX_DIM) ** -0.5
    w_idx = w_in[:, o_wi:o_u] * idx_scale

    def head_pad(w, heads, before, after):
        w = jnp.pad(w.reshape(d, heads, HEAD_DIM), ((0, 0), (0, 0), (before, after)))
        return w.reshape(d, heads * LANES)

    wq = w_in[:, :a_w]
    t_cols = [head_pad(wq[:, :a_w // 2], A_REP, 0, HEAD_DIM), head_pad(wq[:, a_w // 2:], A_REP, HEAD_DIM, 0),
              w_in[:, o_v:o_qi], head_pad(w_in[:, o_qi:o_ki], IDX_HEADS, 0, LANES - IDX_DIM),
              jnp.pad(w_idx, ((0, 0), (0, WT_ROWS - IDX_HEADS)))]
    wt = jnp.concatenate(t_cols, axis=1).T.astype(BF16)
    kiw = jnp.pad(jnp.concatenate([w_in[:, o_ki:o_wi], w_idx], axis=1),
                  ((0, 0), (0, LANES - IDX_DIM - IDX_HEADS)))
    wc = jnp.concatenate([w_in[:, o_k:o_v], kiw, w_in[:, o_u:]], axis=1).astype(BF16)
    gq = jnp.pad((q_g * (HEAD_DIM ** -0.5))[None, :], ((0, 0), (0, HEAD_DIM)))
    qg = jnp.concatenate([jnp.tile(gq, (A_REP, 1)), jnp.tile(jnp.roll(gq, HEAD_DIM, axis=1), (A_REP, 1))])
    qg_rows = jnp.broadcast_to(qg.reshape(QP_W)[:, None], (QP_W, LANES))
    return wc, wt, qg_rows, jnp.tile(k_g, A_KV_HEADS)[None, :]


def _odd_in_weights(w_qkv):
    width = w_qkv.shape[1] // 3
    wk = w_qkv[:, width:2 * width].astype(BF16)
    wt = jnp.concatenate([w_qkv[:, :width], w_qkv[:, 2 * width:]], axis=1).T.astype(BF16)
    return wk, wt


def _trunk(x, norm_mix_g, norm_ffn_g, ev_w_in, ev_w_out, ev_q_norm_g, ev_k_norm_g, ev_pool_w,
           ev_pool_scale, od_w_qkv, od_w_out, ffn_w1, ffn_w3, ffn_w2, *, tm, fc):
    batch, seq, d = x.shape
    depth = norm_mix_g.shape[0]
    xf = x.reshape(batch * seq, d)
    for layer in range(depth):
        li = layer // 2
        g_mix = norm_mix_g[layer][None, :]
        g_ffn = norm_ffn_g[layer][None, :]
        w1, w3, w2 = (w[layer].astype(BF16) for w in (ffn_w1, ffn_w3, ffn_w2))
        if layer % 2 == 0:
            wc, wt, qg, kg = _even_in_weights(ev_w_in[li], ev_q_norm_g[li], ev_k_norm_g[li])
            qpT, k, vaugT, qipT, kiw, wrow, u = _even_proj(xf, g_mix, wc, wt, qg, kg, tm)
            bound = (HEAD_DIM ** 0.5) * jnp.max(jnp.abs(ev_q_norm_g[li])) * jnp.max(jnp.abs(ev_k_norm_g[li]))
            bound = (ROUNDING_SLACK * bound).astype(F32).reshape(1)
            attn = _dsa_attention(bound, qpT, k, vaugT, qipT, kiw, wrow, batch, seq)
            pool = (u, ev_pool_w[li].astype(BF16), ev_pool_scale[li][None, :])
            xf = _post(xf, attn, ev_w_out[li].astype(BF16), g_ffn, w1, w3, w2, tm, fc, seq, pool=pool)
        else:
            wk, wt = _odd_in_weights(od_w_qkv[li])
            qT, k, vT = _odd_proj(xf, g_mix, wk, wt, tm)
            attn = _sb_attention(qT, k, vT, batch, seq)
            xf = _post(xf, attn, od_w_out[li].astype(BF16), g_ffn, w1, w3, w2, tm, fc, seq)
    return xf.reshape(batch, seq, d)


def kernel(x, norm_mix_g, norm_ffn_g, ev_w_in, ev_w_out, ev_q_norm_g, ev_k_norm_g, ev_pool_w,
           ev_pool_scale, od_w_qkv, od_w_out, ffn_w1, ffn_w3, ffn_w2):
    return _trunk(x, norm_mix_g, norm_ffn_g, ev_w_in, ev_w_out, ev_q_norm_g, ev_k_norm_g, ev_pool_w,
                  ev_pool_scale, od_w_qkv, od_w_out, ffn_w1, ffn_w3, ffn_w2, tm=512, fc=2816)
```

```python
import functools

import jax
import jax.numpy as jnp
import numpy as np
from jax import lax
from jax.experimental import pallas as pl
from jax.experimental.pallas import tpu as pltpu

F32 = jnp.float32
BF16 = jnp.bfloat16

HEAD_DIM = 64
A_HEADS = 8
A_KV_HEADS = 2
A_REP = A_HEADS // A_KV_HEADS
IDX_HEADS = 4
IDX_DIM = 64
TOPK_MAX = 256
POOL_WINDOWS = (2, 4, 8, 16)
POOL_GROUP = 128
POOL_WIDTH = POOL_GROUP * len(POOL_WINDOWS)
POOL_HALO = 16
EPS = 1e-6

LANES = 128
SUBLANES = 8
KEY_CHUNK = 256
INT_MIN = np.int32(-(2**31))
MASK_BIAS = -1e30
MAX_CONST_SHIFT = 20.0
ROUNDING_SLACK = 1.01
EXP_UNDERFLOW = 105.0
SB_TAIL = 192
VMEM_LIMIT = 56 * 1024 * 1024

_NT = (((1,), (1,)), ((), ()))


def _dot(a, b):
    return jnp.dot(a, b, preferred_element_type=F32)


def _dot_nt(a, b):
    return lax.dot_general(a, b, _NT, preferred_element_type=F32)


def _split_bf16(x):
    hi = x.astype(BF16)
    lo = (x - hi.astype(F32)).astype(BF16)
    return hi, lo


def _rms_rows(x, g):
    ms = jnp.mean(x * x, axis=-1, keepdims=True)
    return x * lax.rsqrt(ms + EPS) * g


def _fold_rows(x, op):
    parts = [x[r:r + SUBLANES, :] for r in range(0, x.shape[0], SUBLANES)]
    while len(parts) > 1:
        nxt = [op(parts[a], parts[a + 1]) for a in range(0, len(parts) - 1, 2)]
        if len(parts) % 2:
            nxt.append(parts[-1])
        parts = nxt
    return parts[0]


def _const_spec(shape):
    nd = len(shape)
    return pl.BlockSpec(shape, lambda *_: (0,) * nd, pipeline_mode=pl.Buffered(1))


def _params(sem):
    return pltpu.CompilerParams(dimension_semantics=sem, vmem_limit_bytes=VMEM_LIMIT)


QB = KEY_CHUNK
QP_W = A_HEADS * LANES
QIP_W = IDX_HEADS * LANES
WT_ROWS = 16
_ROW_V = QP_W
_ROW_QI = _ROW_V + LANES
_ROW_W = _ROW_QI + QIP_W
EVEN_T_ROWS = _ROW_W + WT_ROWS
_COL_KIW = LANES
_COL_U = 2 * LANES
EVEN_COLS = _COL_U + POOL_WIDTH


def _even_proj_kernel(x_ref, g_ref, wc_ref, wt_ref, qg_ref, kg_ref, seg_ref,
                      qpT_ref, k_ref, vaugT_ref, qipT_ref, kiw_ref, wrow_ref, u_ref):
    h = _rms_rows(x_ref[...], g_ref[...]).astype(BF16)
    tm = h.shape[0]
    reps = tm // LANES
    blocks = tm // QB
    yT = _dot_nt(wt_ref[...], h)

    for hh in range(A_HEADS):
        rows = slice(hh * LANES, (hh + 1) * LANES)
        y = yT[rows, :]
        ss = jnp.sum(y * y, axis=0, keepdims=True)
        gain = jnp.concatenate([qg_ref[rows, :]] * reps, axis=1)
        qT = (y * lax.rsqrt(ss * (1.0 / HEAD_DIM) + EPS) * gain).astype(BF16)
        g, r = divmod(hh, A_REP)
        for j in range(blocks):
            qpT_ref[j, g, :, r * QB:(r + 1) * QB] = qT[:, j * QB:(j + 1) * QB]

    yc = _dot(h, wc_ref[...])
    y = yc[:, 0:LANES]
    hi, lo = _split_bf16(y * y)
    ss = _dot(hi, seg_ref[...]) + _dot(lo, seg_ref[...])
    k_ref[...] = (y * lax.rsqrt(ss * (1.0 / HEAD_DIM) + EPS) * kg_ref[...]).astype(BF16)

    vT = yT[_ROW_V:_ROW_V + LANES, :]
    one = jnp.ones((HEAD_DIM, tm), F32)
    aug = jnp.concatenate([vT[0:HEAD_DIM], one, one, vT[HEAD_DIM:LANES]], axis=0).astype(BF16)
    for j in range(tm // KEY_CHUNK):
        vaugT_ref[j] = aug[:, j * KEY_CHUNK:(j + 1) * KEY_CHUNK]

    for hh in range(IDX_HEADS):
        qiT = yT[_ROW_QI + hh * LANES:_ROW_QI + (hh + 1) * LANES, :].astype(BF16)
        for j in range(blocks):
            qipT_ref[j, :, hh * QB:(hh + 1) * QB] = qiT[:, j * QB:(j + 1) * QB]
            wrow_ref[j, :, hh * QB:(hh + 1) * QB] = jnp.broadcast_to(
                yT[_ROW_W + hh:_ROW_W + hh + 1, j * QB:(j + 1) * QB], (SUBLANES, QB))
    kiw_ref[...] = yc[:, _COL_KIW:_COL_KIW + LANES].astype(BF16)
    u_ref[...] = yc[:, _COL_U:_COL_U + POOL_WIDTH]


def _even_proj(x, g, wc, wt, qg, kg, tm):
    n, d = x.shape
    r = np.arange(LANES) // HEAD_DIM
    seg = jnp.asarray(r[:, None] == r[None, :], BF16)
    row = lambda wd: pl.BlockSpec((tm, wd), lambda i: (i, 0))
    cpt = tm // KEY_CHUNK
    bpt = tm // QB
    out_specs = [pl.BlockSpec((bpt, A_KV_HEADS, LANES, A_REP * QB), lambda i: (i, 0, 0, 0)), row(LANES),
                 pl.BlockSpec((cpt, 2 * LANES, KEY_CHUNK), lambda i: (i, 0, 0)),
                 pl.BlockSpec((bpt, LANES, IDX_HEADS * QB), lambda i: (i, 0, 0)), row(LANES),
                 pl.BlockSpec((bpt, SUBLANES, IDX_HEADS * QB), lambda i: (i, 0, 0)), row(POOL_WIDTH)]
    out_shape = [jax.ShapeDtypeStruct((n // QB, A_KV_HEADS, LANES, A_REP * QB), BF16),
                 jax.ShapeDtypeStruct((n, LANES), BF16),
                 jax.ShapeDtypeStruct((n // KEY_CHUNK, 2 * LANES, KEY_CHUNK), BF16),
                 jax.ShapeDtypeStruct((n // QB, LANES, IDX_HEADS * QB), BF16),
                 jax.ShapeDtypeStruct((n, LANES), BF16),
                 jax.ShapeDtypeStruct((n // QB, SUBLANES, IDX_HEADS * QB), F32),
                 jax.ShapeDtypeStruct((n, POOL_WIDTH), F32)]
    return pl.pallas_call(
        _even_proj_kernel,
        grid=(n // tm,),
        in_specs=[row(d), _const_spec((1, d)), _const_spec(wc.shape), _const_spec(wt.shape),
                  _const_spec(qg.shape), _const_spec((1, LANES)), _const_spec((LANES, LANES))],
        out_specs=out_specs,
        out_shape=out_shape,
        compiler_params=_params(("parallel",)),
        name="even_proj",
    )(x, g, wc, wt, qg, kg, seg)


def _dsa_kernel(shift_ref, qpT_ref, qipT_ref, wrow_ref, k_ref, kiw_ref, vaugT_ref, lower_ref, o_ref,
                keys_ref, hi_ref, lo_ref, bias_ref, acc_ref, *, n_sel):
    i = pl.program_id(1)
    nk = i + 1
    kpos = lax.broadcasted_iota(jnp.int32, (KEY_CHUNK, QB), 0)
    qpos = lax.broadcasted_iota(jnp.int32, (KEY_CHUNK, QB), 1)
    causal = kpos <= qpos
    logit_bound = shift_ref[0]
    small_bound = logit_bound <= MAX_CONST_SHIFT
    sel_bias = jnp.where(small_bound, -logit_bound, 0.0)

    def chunk_rows(c):
        return pl.ds(pl.multiple_of(c * KEY_CHUNK, KEY_CHUNK), KEY_CHUNK)

    def per_head(x, n):
        return [x[:, r * QB:(r + 1) * QB] for r in range(n)]

    def score_matmul(c):
        return _dot(kiw_ref[chunk_rows(c), :], qipT_ref[0])

    def score_chunk(c, diagonal, d=None):
        d = score_matmul(c) if d is None else d
        terms = per_head(jnp.maximum(d, 0.0) * wrow_ref[0, 0:1, :], IDX_HEADS)
        s = (terms[0] + terms[1]) + (terms[2] + terms[3])
        bits = lax.bitcast_convert_type(s, jnp.int32)
        key = bits ^ ((bits >> 31) & jnp.int32(0x7FFFFFFF))
        if diagonal:
            key = jnp.where(causal, key, INT_MIN)
        keys_ref[c] = key
        half = pl.ds(pl.multiple_of((c & 1) * KEY_CHUNK, KEY_CHUNK), KEY_CHUNK)
        hi_ref[c >> 1, half, :] = (key >> 16).astype(jnp.int16)
        lo_ref[c >> 1, half, :] = ((key & 0xFFFF) - 32768).astype(jnp.int16)

    def score_pair(p, carry):
        ds = [score_matmul(2 * p), score_matmul(2 * p + 1)]
        score_chunk(2 * p, False, ds[0])
        score_chunk(2 * p + 1, False, ds[1])
        return carry

    lax.fori_loop(0, i >> 1, score_pair, 0)

    @pl.when((i & 1) == 1)
    def _():
        score_chunk(i - 1, False)

    score_chunk(i, True)

    npair = (nk + 1) >> 1

    @pl.when((nk & 1) == 1)
    def _():
        pad = jnp.full((KEY_CHUNK, QB), -32768, jnp.int16)
        hi_ref[nk >> 1, KEY_CHUNK:2 * KEY_CHUNK, :] = pad
        lo_ref[nk >> 1, KEY_CHUNK:2 * KEY_CHUNK, :] = pad

    def count16(ref, pred):
        def body(c, acc):
            m = jnp.where(pred(ref[c]), jnp.int16(1), jnp.int16(0))
            parts = [m[r:r + 2 * SUBLANES, :] for r in range(0, 2 * KEY_CHUNK, 2 * SUBLANES)]
            while len(parts) > 1:
                parts = [parts[a] + parts[a + 1] for a in range(0, len(parts), 2)]
            return acc + parts[0]
        acc = lax.fori_loop(0, npair, body, jnp.zeros((2 * SUBLANES, QB), jnp.int16))
        return jnp.sum(acc.astype(jnp.int32), axis=0, keepdims=True)

    def bisect16(ref, target):
        def step(it, thr):
            bit = jnp.where(it == 0, jnp.int32(-32768), lax.shift_left(jnp.int32(1), 15 - it))
            cand = thr ^ bit
            cand16 = cand.astype(jnp.int16)
            cnt = count16(ref, lambda kc: kc >= cand16)
            return jnp.where(cnt >= target, cand, thr)
        return lax.fori_loop(0, 16, step, jnp.full((1, QB), -32768, jnp.int32))

    thr_hi = bisect16(hi_ref, n_sel)
    thr_hi16 = thr_hi.astype(jnp.int16)
    need_lo = n_sel - count16(hi_ref, lambda kc: kc > thr_hi16)

    def bucket_body(c, carry):
        lo_ref[c] = jnp.where(hi_ref[c] == thr_hi16, lo_ref[c], jnp.int16(-32768))
        return carry

    lax.fori_loop(0, npair, bucket_body, 0)
    thr_lo = bisect16(lo_ref, need_lo)
    thr = lax.shift_left(thr_hi, 16) | (thr_lo + 32768)
    thr = jnp.maximum(thr, INT_MIN + 1)

    def count(pred):
        def body(c, acc):
            return acc + _fold_rows(jnp.where(pred(keys_ref[c], c), 1, 0), jnp.add)
        acc = lax.fori_loop(0, nk, body, jnp.zeros((SUBLANES, QB), jnp.int32))
        return jnp.sum(acc, axis=0, keepdims=True)

    cnt_ge = count(lambda kc, c: kc >= thr)
    overflow = jnp.max(jnp.where(cnt_ge > n_sel, 1, 0))

    @pl.when(overflow == 0)
    def _():
        def body(c, carry):
            bias_ref[c] = jnp.where(keys_ref[c] >= thr, sel_bias, MASK_BIAS)
            return carry
        lax.fori_loop(0, nk, body, 0)

    @pl.when(overflow != 0)
    def _():
        need = (n_sel - count(lambda kc, c: kc > thr)).astype(F32)

        def rank_chunks(chunks, earlier):
            ones = [jnp.where(keys_ref[c] == thr, 1.0, 0.0) for c in chunks]
            ranks = [_dot(lower_ref[...], o.astype(BF16)) for o in ones]
            for c, o, rank in zip(chunks, ones, ranks):
                kc = keys_ref[c]
                tie_bias = jnp.where(rank + earlier < need, sel_bias, MASK_BIAS)
                bias_ref[c] = jnp.where(kc > thr, sel_bias, jnp.where(kc == thr, tie_bias, MASK_BIAS))
                earlier = earlier + jnp.sum(_fold_rows(o, jnp.add), axis=0, keepdims=True)
            return earlier

        earlier = lax.fori_loop(0, nk >> 1, lambda p, e: rank_chunks([2 * p, 2 * p + 1], e),
                                jnp.zeros((1, QB), F32))

        @pl.when((nk & 1) == 1)
        def _():
            rank_chunks([nk - 1], earlier)

    acc_ref[...] = jnp.zeros_like(acc_ref)

    def group_logits(c, g):
        bias = bias_ref[c]
        return _dot(k_ref[chunk_rows(c), :], qpT_ref[0, g]) + jnp.concatenate([bias] * A_REP, axis=1)

    def attend(shifts):
        def attn_chunks(chunks):
            xs = [[group_logits(c, g) for g in range(A_KV_HEADS)] for c in chunks]
            for c, xc in zip(chunks, xs):
                for g in range(A_KV_HEADS):
                    x = xc[g]
                    if shifts is not None:
                        x = x - shifts[g]
                    p = jnp.exp(x).astype(BF16)
                    acc_ref[g] += _dot(vaugT_ref[c, g * LANES:(g + 1) * LANES, :], p)

        def attn_pair(p, carry):
            attn_chunks([2 * p, 2 * p + 1])
            return carry
        lax.fori_loop(0, nk >> 1, attn_pair, 0)

        @pl.when((nk & 1) == 1)
        def _():
            attn_chunks([nk - 1])

    @pl.when(small_bound)
    def _():
        attend(None)

    @pl.when(jnp.logical_not(small_bound))
    def _():
        def max_chunk(c, maxes):
            return tuple(jnp.maximum(maxes[g], _fold_rows(group_logits(c, g), jnp.maximum))
                         for g in range(A_KV_HEADS))
        start = (jnp.full((SUBLANES, A_REP * QB), MASK_BIAS, F32),) * A_KV_HEADS
        maxes = lax.fori_loop(0, nk, max_chunk, start)
        attend([jnp.max(m, axis=0, keepdims=True) for m in maxes])

    for pr in range(A_HEADS // 2):
        parts = []
        for hh in (2 * pr, 2 * pr + 1):
            g, r = divmod(hh, A_REP)
            a = acc_ref[g, :, r * QB:(r + 1) * QB]
            lo, hi = a[0:HEAD_DIM], a[HEAD_DIM:LANES]
            parts.append(lo / hi if g == 0 else hi / lo)
        o_ref[:, pr * LANES:(pr + 1) * LANES] = jnp.concatenate(parts, axis=0).T.astype(BF16)


def _dsa_attention(logit_bound, qpT, k, vaugT, qipT, kiw, wrow, batch, seq):
    n = k.shape[0]
    nq = seq // QB
    nc = seq // KEY_CHUNK
    n_sel = min(TOPK_MAX, seq // 4)
    r = np.arange(KEY_CHUNK)
    lower = jnp.asarray(r[None, :] < r[:, None], BF16)
    full = lambda wd: pl.BlockSpec((seq, wd), lambda b, i: (b, 0))
    kern = functools.partial(_dsa_kernel, n_sel=n_sel)
    return pl.pallas_call(
        kern,
        grid=(batch, nq),
        in_specs=[pl.BlockSpec(memory_space=pltpu.SMEM),
                  pl.BlockSpec((1, A_KV_HEADS, LANES, A_REP * QB), lambda b, i: (b * nq + i, 0, 0, 0)),
                  pl.BlockSpec((1, LANES, IDX_HEADS * QB), lambda b, i: (b * nq + i, 0, 0)),
                  pl.BlockSpec((1, SUBLANES, IDX_HEADS * QB), lambda b, i: (b * nq + i, 0, 0)),
                  full(LANES), full(LANES), pl.BlockSpec((nc, 2 * LANES, KEY_CHUNK), lambda b, i: (b, 0, 0)),
                  _const_spec(lower.shape)],
        out_specs=pl.BlockSpec((QB, A_HEADS * HEAD_DIM), lambda b, i: (b * nq + i, 0)),
        out_shape=jax.ShapeDtypeStruct((n, A_HEADS * HEAD_DIM), BF16),
        scratch_shapes=[pltpu.VMEM((nc, KEY_CHUNK, QB), jnp.int32),
                        pltpu.VMEM((nc // 2, 2 * KEY_CHUNK, QB), jnp.int16),
                        pltpu.VMEM((nc // 2, 2 * KEY_CHUNK, QB), jnp.int16), pltpu.VMEM((nc, KEY_CHUNK, QB), F32),
                        pltpu.VMEM((A_KV_HEADS, LANES, A_REP * QB), F32)],
        compiler_params=_params(("parallel", "arbitrary")),
        name="dsa_attention",
    )(logit_bound, qpT, qipT, wrow, k, kiw, vaugT, lower)


def _odd_proj_kernel(x_ref, g_ref, wk_ref, wt_ref, qT_ref, k_ref, vT_ref):
    h = _rms_rows(x_ref[...], g_ref[...]).astype(BF16)
    tm = h.shape[0]
    width = wk_ref.shape[1]
    top_rows = lax.broadcasted_iota(jnp.int32, (LANES, tm), 0) < HEAD_DIM
    k_ref[...] = _dot(h, wk_ref[...]).astype(BF16)
    yT = _dot_nt(wt_ref[...], h)
    for pp in range(width // LANES):
        y = yT[pp * LANES:(pp + 1) * LANES, :] * (HEAD_DIM ** -0.5)
        first = jnp.where(top_rows, y, 0.0).astype(BF16)
        second = jnp.where(top_rows, 0.0, y).astype(BF16)
        for j in range(tm // QB):
            qT_ref[j, pp, :, 0:QB] = first[:, j * QB:(j + 1) * QB]
            qT_ref[j, pp, :, QB:2 * QB] = second[:, j * QB:(j + 1) * QB]
    vT = yT[width:2 * width, :].astype(BF16)
    for j in range(tm // KEY_CHUNK):
        vT_ref[j] = vT[:, j * KEY_CHUNK:(j + 1) * KEY_CHUNK]


def _odd_proj(x, g, wk, wt, tm):
    n, d = x.shape
    width = wk.shape[1]
    pairs = width // LANES
    return pl.pallas_call(
        _odd_proj_kernel,
        grid=(n // tm,),
        in_specs=[pl.BlockSpec((tm, d), lambda i: (i, 0)), _const_spec((1, d)), _const_spec(wk.shape),
                  _const_spec(wt.shape)],
        out_specs=[pl.BlockSpec((tm // QB, pairs, LANES, 2 * QB), lambda i: (i, 0, 0, 0)),
                   pl.BlockSpec((tm, width), lambda i: (i, 0)),
                   pl.BlockSpec((tm // KEY_CHUNK, width, KEY_CHUNK), lambda i: (i, 0, 0))],
        out_shape=[jax.ShapeDtypeStruct((n // QB, pairs, LANES, 2 * QB), BF16),
                   jax.ShapeDtypeStruct((n, width), BF16),
                   jax.ShapeDtypeStruct((n // KEY_CHUNK, width, KEY_CHUNK), BF16)],
        compiler_params=_params(("parallel",)),
        name="odd_proj",
    )(x, g, wk, wt)


def _sb_kernel(qT_ref, k_ref, vT_ref, tri_ref, o_ref, acc_ref, carry_ref):
    i = pl.program_id(2)
    q = qT_ref[0, 0]

    krel = lax.broadcasted_iota(jnp.int32, (KEY_CHUNK, 2 * QB), 0)
    qrel = lax.broadcasted_iota(jnp.int32, (KEY_CHUNK, 2 * QB), 1) & (QB - 1)
    strict = krel < qrel

    def masked(x, diagonal):
        if not diagonal:
            return x
        rows = x.shape[0] - KEY_CHUNK
        last = jnp.where(strict, x[rows:], 0.0)
        return last if rows == 0 else jnp.concatenate([x[:rows], last], axis=0)

    def block(segs, carry, diagonal):
        chunk0, off0, _ = segs[0]
        start = pl.multiple_of(chunk0 * KEY_CHUNK, KEY_CHUNK) + off0
        bounds = [sum(r for _, _, r in segs[:n]) for n in range(len(segs) + 1)]
        z = _dot(k_ref[pl.ds(start, bounds[-1]), :], q)
        l1p = jnp.log(1.0 + jnp.exp(-jnp.abs(z)))
        sp = jnp.maximum(z, 0.0) + l1p
        ls = z - sp
        sp = masked(sp, diagonal)
        sums = [jnp.sum(_fold_rows(sp[bounds[n]:bounds[n + 1]], jnp.add), axis=0, keepdims=True)
                for n in range(len(segs))]
        pv = None
        for n, (chunk, off, rows) in enumerate(segs):
            later = _dot(tri_ref[0:rows, 0:rows], sp[bounds[n]:bounds[n + 1]].astype(BF16)) + carry
            for nxt in range(n + 1, len(segs)):
                later = later + sums[nxt]
            a = jnp.exp(ls[bounds[n]:bounds[n + 1]] - later)
            if diagonal and n == len(segs) - 1:
                a = jnp.where(strict, a, 0.0)
            parts = [a.astype(BF16)]
            if off:
                parts.insert(0, jnp.zeros((off, 2 * QB), BF16))
            if off + rows < KEY_CHUNK:
                parts.append(jnp.zeros((KEY_CHUNK - off - rows, 2 * QB), BF16))
            term = _dot(vT_ref[chunk], parts[0] if len(parts) == 1 else jnp.concatenate(parts, axis=0))
            pv = term if pv is None else pv + term
        acc_ref[...] += pv
        for n in range(len(segs)):
            carry = carry + sums[n]
        return carry

    def unfinished(carry):
        return (jnp.min(carry) <= EXP_UNDERFLOW).astype(jnp.int32)

    acc_ref[...] = jnp.zeros_like(acc_ref)
    zero = jnp.zeros((1, 2 * QB), F32)
    carry_ref[...] = zero

    @pl.when(i == 0)
    def _():
        block([(0, 0, KEY_CHUNK)], zero, True)

    @pl.when(i > 0)
    def _():
        carry = block([(i - 1, KEY_CHUNK - SB_TAIL, SB_TAIL), (i, 0, KEY_CHUNK)], zero, True)
        carry_ref[...] = carry

        @pl.when(unfinished(carry) > 0)
        def _():
            carry_ref[...] = block([(i - 1, 0, KEY_CHUNK - SB_TAIL)], carry, False)

    def cond(st):
        return jnp.logical_and(st[0] >= 0, st[1] > 0)

    def body(st):
        carry = block([(st[0], 0, KEY_CHUNK)], st[2], False)
        return st[0] - 1, unfinished(carry), carry

    carry = carry_ref[...]
    lax.while_loop(cond, body, (i - 2, unfinished(carry), carry))
    oT = jnp.concatenate([acc_ref[0:HEAD_DIM, 0:QB], acc_ref[HEAD_DIM:LANES, QB:2 * QB]], axis=0)
    o_ref[...] = oT.T.astype(BF16)


def _sb_attention(qT, k, vT, batch, seq):
    n, width = k.shape
    pairs = width // LANES
    nq = seq // QB
    nc = seq // KEY_CHUNK
    r = np.arange(KEY_CHUNK)
    tri = jnp.asarray(r[None, :] > r[:, None], BF16)
    return pl.pallas_call(
        _sb_kernel,
        grid=(batch, pairs, nq),
        in_specs=[pl.BlockSpec((1, 1, LANES, 2 * QB), lambda b, p, i: (b * nq + i, p, 0, 0)),
                  pl.BlockSpec((seq, LANES), lambda b, p, i: (b, p)),
                  pl.BlockSpec((nc, LANES, KEY_CHUNK), lambda b, p, i: (b, p, 0)),
                  _const_spec(tri.shape)],
        out_specs=pl.BlockSpec((QB, LANES), lambda b, p, i: (b * nq + i, p)),
        out_shape=jax.ShapeDtypeStruct((n, width), BF16),
        scratch_shapes=[pltpu.VMEM((LANES, 2 * QB), F32), pltpu.VMEM((1, 2 * QB), F32)],
        compiler_params=_params(("parallel", "parallel", "arbitrary")),
        name="sb_attention",
    )(qT, k, vT, tri)


def _pooled(u_ref, halo_ref, pw_ref, ps_ref, ext_ref, seq, tm):
    tpos0 = (pl.program_id(0) * tm) % seq
    halo = jnp.where(tpos0 == 0, 0.0, halo_ref[...])
    ext_ref[0:POOL_HALO, :] = halo
    ext_ref[POOL_HALO:POOL_HALO + tm, :] = u_ref[...]
    t_idx = lax.broadcasted_iota(jnp.int32, (tm, POOL_GROUP), 0) + tpos0
    out = []
    for gi, w in enumerate(POOL_WINDOWS):
        sl = slice(gi * POOL_GROUP, (gi + 1) * POOL_GROUP)
        s = ext_ref[POOL_HALO:POOL_HALO + tm, sl]
        for dlt in range(1, w):
            s = s + ext_ref[POOL_HALO - dlt:POOL_HALO - dlt + tm, sl]
        cnt = jnp.minimum(t_idx + 1, w).astype(F32)
        mixed = s / cnt - u_ref[:, sl]
        y = _dot(mixed.astype(BF16), pw_ref[gi]) * ps_ref[:, sl]
        out.append(y.astype(BF16))
    return out


def _post_kernel(*refs, even, seq, tm):
    if even:
        (attn_ref, u_ref, halo_ref, pw_ref, ps_ref, wo_ref, x_ref, g_ref, w1_ref, w3_ref, w2_ref,
         o_ref, ext_ref) = refs
        half = attn_ref.shape[1]
        y = _dot(attn_ref[...], wo_ref[0:half, :])
        for gi, pg in enumerate(_pooled(u_ref, halo_ref, pw_ref, ps_ref, ext_ref, seq, tm)):
            y = y + _dot(pg, wo_ref[half + gi * POOL_GROUP:half + (gi + 1) * POOL_GROUP, :])
    else:
        attn_ref, wo_ref, x_ref, g_ref, w1_ref, w3_ref, w2_ref, o_ref = refs
        y = _dot(attn_ref[...], wo_ref[...])
    x1 = x_ref[...] + y
    h = _rms_rows(x1, g_ref[...]).astype(BF16)
    a = _dot(h, w1_ref[...])
    b = _dot(h, w3_ref[...])
    gate = (a * (1.0 / (1.0 + jnp.exp(-a))) * b).astype(BF16)
    o_ref[...] = x1 + _dot(gate, w2_ref[...])


def _post(x, attn, w_out, g, w1, w3, w2, tm, seq, pool=None):
    n, d = x.shape
    even = pool is not None
    row = lambda wd: pl.BlockSpec((tm, wd), lambda i: (i, 0))
    ffn_specs = [_const_spec(w_out.shape), row(d), _const_spec((1, d)),
                 _const_spec(w1.shape), _const_spec(w3.shape), _const_spec(w2.shape)]
    scratch = []
    if even:
        u, pool_w, pool_scale = pool
        halo_blocks = tm // POOL_HALO
        halo = pl.BlockSpec((POOL_HALO, u.shape[1]), lambda i: (jnp.maximum(i * halo_blocks - 1, 0), 0))
        in_specs = [row(attn.shape[1]), row(u.shape[1]), halo, _const_spec(pool_w.shape),
                    _const_spec(pool_scale.shape)] + ffn_specs
        args = (attn, u, u, pool_w, pool_scale, w_out, x, g, w1, w3, w2)
        scratch.append(pltpu.VMEM((POOL_HALO + tm, u.shape[1]), F32))
    else:
        in_specs = [row(attn.shape[1])] + ffn_specs
        args = (attn, w_out, x, g, w1, w3, w2)
    kern = functools.partial(_post_kernel, even=even, seq=seq, tm=tm)
    return pl.pallas_call(
        kern,
        grid=(n // tm,),
        in_specs=in_specs,
        out_specs=row(d),
        out_shape=jax.ShapeDtypeStruct((n, d), F32),
        scratch_shapes=scratch,
        compiler_params=_params(("parallel",)),
        name="post_even" if even else "post_odd",
    )(*args)


def _even_in_weights(w_in, q_g, k_g):
    d = w_in.shape[0]
    a_w, kv_w, qi_w = A_HEADS * HEAD_DIM, A_KV_HEADS * HEAD_DIM, IDX_HEADS * IDX_DIM
    o_k, o_v, o_qi = a_w, a_w + kv_w, a_w + 2 * kv_w
    o_ki = o_qi + qi_w
    o_wi = o_ki + IDX_DIM
    o_u = o_wi + IDX_HEADS
    idx_scale = (IDX_HEADS * IDX_DIM) ** -0.5
    w_idx = w_in[:, o_wi:o_u] * idx_scale

    def head_pad(w, heads, before, after):
        w = jnp.pad(w.reshape(d, heads, HEAD_DIM), ((0, 0), (0, 0), (before, after)))
        return w.reshape(d, heads * LANES)

    wq = w_in[:, :a_w]
    t_cols = [head_pad(wq[:, :a_w // 2], A_REP, 0, HEAD_DIM), head_pad(wq[:, a_w // 2:], A_REP, HEAD_DIM, 0),
              w_in[:, o_v:o_qi], head_pad(w_in[:, o_qi:o_ki], IDX_HEADS, 0, LANES - IDX_DIM),
              jnp.pad(w_idx, ((0, 0), (0, WT_ROWS - IDX_HEADS)))]
    wt = jnp.concatenate(t_cols, axis=1).T.astype(BF16)
    kiw = jnp.pad(jnp.concatenate([w_in[:, o_ki:o_wi], w_idx], axis=1),
                  ((0, 0), (0, LANES - IDX_DIM - IDX_HEADS)))
    wc = jnp.concatenate([w_in[:, o_k:o_v], kiw, w_in[:, o_u:]], axis=1).astype(BF16)
    gq = jnp.pad((q_g * (HEAD_DIM ** -0.5))[None, :], ((0, 0), (0, HEAD_DIM)))
    qg = jnp.concatenate([jnp.tile(gq, (A_REP, 1)), jnp.tile(jnp.roll(gq, HEAD_DIM, axis=1), (A_REP, 1))])
    qg_rows = jnp.broadcast_to(qg.reshape(QP_W)[:, None], (QP_W, LANES))
    return wc, wt, qg_rows, jnp.tile(k_g, A_KV_HEADS)[None, :]


def _odd_in_weights(w_qkv):
    width = w_qkv.shape[1] // 3
    wk = w_qkv[:, width:2 * width].astype(BF16)
    wt = jnp.concatenate([w_qkv[:, :width], w_qkv[:, 2 * width:]], axis=1).T.astype(BF16)
    return wk, wt


def _trunk(x, norm_mix_g, norm_ffn_g, ev_w_in, ev_w_out, ev_q_norm_g, ev_k_norm_g, ev_pool_w,
           ev_pool_scale, od_w_qkv, od_w_out, ffn_w1, ffn_w3, ffn_w2, *, tm):
    batch, seq, d = x.shape
    depth = norm_mix_g.shape[0]
    xf = x.reshape(batch * seq, d)
    for layer in range(depth):
        li = layer // 2
        g_mix = norm_mix_g[layer][None, :]
        g_ffn = norm_ffn_g[layer][None, :]
        w1, w3, w2 = (w[layer].astype(BF16) for w in (ffn_w1, ffn_w3, ffn_w2))
        if layer % 2 == 0:
            wc, wt, qg, kg = _even_in_weights(ev_w_in[li], ev_q_norm_g[li], ev_k_norm_g[li])
            qpT, k, vaugT, qipT, kiw, wrow, u = _even_proj(xf, g_mix, wc, wt, qg, kg, tm)
            bound = (HEAD_DIM ** 0.5) * jnp.max(jnp.abs(ev_q_norm_g[li])) * jnp.max(jnp.abs(ev_k_norm_g[li]))
            bound = (ROUNDING_SLACK * bound).astype(F32).reshape(1)
            attn = _dsa_attention(bound, qpT, k, vaugT, qipT, kiw, wrow, batch, seq)
            pool = (u, ev_pool_w[li].astype(BF16), ev_pool_scale[li][None, :])
            xf = _post(xf, attn, ev_w_out[li].astype(BF16), g_ffn, w1, w3, w2, tm, seq, pool=pool)
        else:
            wk, wt = _odd_in_weights(od_w_qkv[li])
            qT, k, vT = _odd_proj(xf, g_mix, wk, wt, tm)
            attn = _sb_attention(qT, k, vT, batch, seq)
            xf = _post(xf, attn, od_w_out[li].astype(BF16), g_ffn, w1, w3, w2, tm, seq)
    return xf.reshape(batch, seq, d)


def kernel(x, norm_mix_g, norm_ffn_g, ev_w_in, ev_w_out, ev_q_norm_g, ev_k_norm_g, ev_pool_w,
           ev_pool_scale, od_w_qkv, od_w_out, ffn_w1, ffn_w3, ffn_w2):
    return _trunk(x, norm_mix_g, norm_ffn_g, ev_w_in, ev_w_out, ev_q_norm_g, ev_k_norm_g, ev_pool_w,
                  ev_pool_scale, od_w_qkv, od_w_out, ffn_w1, ffn_w3, ffn_w2, tm=512)
```

```python
import functools

import jax
import jax.numpy as jnp
import numpy as np
from jax import lax
from jax.experimental import pallas as pl
from jax.experimental.pallas import tpu as pltpu

F32 = jnp.float32
BF16 = jnp.bfloat16

HEAD_DIM = 64
A_HEADS = 8
A_KV_HEADS = 2
A_REP = A_HEADS // A_KV_HEADS
IDX_HEADS = 4
IDX_DIM = 64
TOPK_MAX = 256
POOL_WINDOWS = (2, 4, 8, 16)
POOL_GROUP = 128
POOL_WIDTH = POOL_GROUP * len(POOL_WINDOWS)
POOL_HALO = 16
EPS = 1e-6

LANES = 128
SUBLANES = 8
KEY_CHUNK = 256
INT_MIN = np.int32(-(2**31))
MASK_BIAS = -1e30
MAX_CONST_SHIFT = 20.0
ROUNDING_SLACK = 1.01
EXP_UNDERFLOW = 105.0
SB_TAIL = 192
VMEM_LIMIT = 56 * 1024 * 1024

_NT = (((1,), (1,)), ((), ()))


def _dot(a, b):
    return jnp.dot(a, b, preferred_element_type=F32)


def _dot_nt(a, b):
    return lax.dot_general(a, b, _NT, preferred_element_type=F32)


def _split_bf16(x):
    hi = x.astype(BF16)
    lo = (x - hi.astype(F32)).astype(BF16)
    return hi, lo


def _rms_rows(x, g):
    ms = jnp.mean(x * x, axis=-1, keepdims=True)
    return x * lax.rsqrt(ms + EPS) * g


def _fold_rows(x, op):
    parts = [x[r:r + SUBLANES, :] for r in range(0, x.shape[0], SUBLANES)]
    while len(parts) > 1:
        nxt = [op(parts[a], parts[a + 1]) for a in range(0, len(parts) - 1, 2)]
        if len(parts) % 2:
            nxt.append(parts[-1])
        parts = nxt
    return parts[0]


def _const_spec(shape):
    nd = len(shape)
    return pl.BlockSpec(shape, lambda *_: (0,) * nd, pipeline_mode=pl.Buffered(1))


def _params(sem):
    return pltpu.CompilerParams(dimension_semantics=sem, vmem_limit_bytes=VMEM_LIMIT)


QB = KEY_CHUNK
QP_W = A_HEADS * LANES
QIP_W = IDX_HEADS * LANES
WT_ROWS = 16
_ROW_V = QP_W
_ROW_QI = _ROW_V + LANES
_ROW_W = _ROW_QI + QIP_W
EVEN_T_ROWS = _ROW_W + WT_ROWS
_COL_KIW = LANES
_COL_U = 2 * LANES
EVEN_COLS = _COL_U + POOL_WIDTH


def _even_proj_kernel(x_ref, g_ref, wc_ref, wt_ref, qg_ref, kg_ref, seg_ref,
                      qpT_ref, k_ref, vaugT_ref, qipT_ref, kiw_ref, wrow_ref, u_ref):
    h = _rms_rows(x_ref[...], g_ref[...]).astype(BF16)
    tm = h.shape[0]
    reps = tm // LANES
    blocks = tm // QB
    yT = _dot_nt(wt_ref[...], h)

    for hh in range(A_HEADS):
        rows = slice(hh * LANES, (hh + 1) * LANES)
        y = yT[rows, :]
        ss = jnp.sum(y * y, axis=0, keepdims=True)
        gain = jnp.concatenate([qg_ref[rows, :]] * reps, axis=1)
        qT = (y * lax.rsqrt(ss * (1.0 / HEAD_DIM) + EPS) * gain).astype(BF16)
        g, r = divmod(hh, A_REP)
        for j in range(blocks):
            qpT_ref[j, g, :, r * QB:(r + 1) * QB] = qT[:, j * QB:(j + 1) * QB]

    yc = _dot(h, wc_ref[...])
    y = yc[:, 0:LANES]
    hi, lo = _split_bf16(y * y)
    ss = _dot(hi, seg_ref[...]) + _dot(lo, seg_ref[...])
    k_ref[...] = (y * lax.rsqrt(ss * (1.0 / HEAD_DIM) + EPS) * kg_ref[...]).astype(BF16)

    vT = yT[_ROW_V:_ROW_V + LANES, :]
    one = jnp.ones((HEAD_DIM, tm), F32)
    aug = jnp.concatenate([vT[0:HEAD_DIM], one, one, vT[HEAD_DIM:LANES]], axis=0).astype(BF16)
    for j in range(tm // KEY_CHUNK):
        vaugT_ref[j] = aug[:, j * KEY_CHUNK:(j + 1) * KEY_CHUNK]

    for hh in range(IDX_HEADS):
        qiT = yT[_ROW_QI + hh * LANES:_ROW_QI + (hh + 1) * LANES, :].astype(BF16)
        for j in range(blocks):
            qipT_ref[j, :, hh * QB:(hh + 1) * QB] = qiT[:, j * QB:(j + 1) * QB]
            wrow_ref[j, :, hh * QB:(hh + 1) * QB] = jnp.broadcast_to(
                yT[_ROW_W + hh:_ROW_W + hh + 1, j * QB:(j + 1) * QB], (SUBLANES, QB))
    kiw_ref[...] = yc[:, _COL_KIW:_COL_KIW + LANES].astype(BF16)
    u_ref[...] = yc[:, _COL_U:_COL_U + POOL_WIDTH]


def _even_proj(x, g, wc, wt, qg, kg, tm):
    n, d = x.shape
    r = np.arange(LANES) // HEAD_DIM
    seg = jnp.asarray(r[:, None] == r[None, :], BF16)
    row = lambda wd: pl.BlockSpec((tm, wd), lambda i: (i, 0))
    cpt = tm // KEY_CHUNK
    bpt = tm // QB
    out_specs = [pl.BlockSpec((bpt, A_KV_HEADS, LANES, A_REP * QB), lambda i: (i, 0, 0, 0)), row(LANES),
                 pl.BlockSpec((cpt, 2 * LANES, KEY_CHUNK), lambda i: (i, 0, 0)),
                 pl.BlockSpec((bpt, LANES, IDX_HEADS * QB), lambda i: (i, 0, 0)), row(LANES),
                 pl.BlockSpec((bpt, SUBLANES, IDX_HEADS * QB), lambda i: (i, 0, 0)), row(POOL_WIDTH)]
    out_shape = [jax.ShapeDtypeStruct((n // QB, A_KV_HEADS, LANES, A_REP * QB), BF16),
                 jax.ShapeDtypeStruct((n, LANES), BF16),
                 jax.ShapeDtypeStruct((n // KEY_CHUNK, 2 * LANES, KEY_CHUNK), BF16),
                 jax.ShapeDtypeStruct((n // QB, LANES, IDX_HEADS * QB), BF16),
                 jax.ShapeDtypeStruct((n, LANES), BF16),
                 jax.ShapeDtypeStruct((n // QB, SUBLANES, IDX_HEADS * QB), F32),
                 jax.ShapeDtypeStruct((n, POOL_WIDTH), F32)]
    return pl.pallas_call(
        _even_proj_kernel,
        grid=(n // tm,),
        in_specs=[row(d), _const_spec((1, d)), _const_spec(wc.shape), _const_spec(wt.shape),
                  _const_spec(qg.shape), _const_spec((1, LANES)), _const_spec((LANES, LANES))],
        out_specs=out_specs,
        out_shape=out_shape,
        compiler_params=_params(("parallel",)),
        name="even_proj",
    )(x, g, wc, wt, qg, kg, seg)


def _dsa_kernel(shift_ref, qpT_ref, qipT_ref, wrow_ref, k_ref, kiw_ref, vaugT_ref, lower_ref, o_ref,
                keys_ref, hi_ref, lo_ref, bias_ref, acc_ref, *, n_sel):
    i = pl.program_id(1)
    nk = i + 1
    kpos = lax.broadcasted_iota(jnp.int32, (KEY_CHUNK, QB), 0)
    qpos = lax.broadcasted_iota(jnp.int32, (KEY_CHUNK, QB), 1)
    causal = kpos <= qpos
    logit_bound = shift_ref[0]
    small_bound = logit_bound <= MAX_CONST_SHIFT
    sel_bias = jnp.where(small_bound, -logit_bound, 0.0)

    def chunk_rows(c):
        return pl.ds(pl.multiple_of(c * KEY_CHUNK, KEY_CHUNK), KEY_CHUNK)

    def per_head(x, n):
        return [x[:, r * QB:(r + 1) * QB] for r in range(n)]

    def score_matmul(c):
        return _dot(kiw_ref[chunk_rows(c), :], qipT_ref[0])

    def score_chunk(c, diagonal, d=None):
        d = score_matmul(c) if d is None else d
        terms = per_head(jnp.maximum(d, 0.0) * wrow_ref[0, 0:1, :], IDX_HEADS)
        s = (terms[0] + terms[1]) + (terms[2] + terms[3])
        bits = lax.bitcast_convert_type(s, jnp.int32)
        key = bits ^ ((bits >> 31) & jnp.int32(0x7FFFFFFF))
        if diagonal:
            key = jnp.where(causal, key, INT_MIN)
        keys_ref[c] = key
        half = pl.ds(pl.multiple_of((c & 1) * KEY_CHUNK, KEY_CHUNK), KEY_CHUNK)
        hi_ref[c >> 1, half, :] = (key >> 16).astype(jnp.int16)
        lo_ref[c >> 1, half, :] = ((key & 0xFFFF) - 32768).astype(jnp.int16)

    def score_pair(p, carry):
        ds = [score_matmul(2 * p), score_matmul(2 * p + 1)]
        score_chunk(2 * p, False, ds[0])
        score_chunk(2 * p + 1, False, ds[1])
        return carry

    lax.fori_loop(0, i >> 1, score_pair, 0)

    @pl.when((i & 1) == 1)
    def _():
        score_chunk(i - 1, False)

    score_chunk(i, True)

    npair = (nk + 1) >> 1

    @pl.when((nk & 1) == 1)
    def _():
        pad = jnp.full((KEY_CHUNK, QB), -32768, jnp.int16)
        hi_ref[nk >> 1, KEY_CHUNK:2 * KEY_CHUNK, :] = pad
        lo_ref[nk >> 1, KEY_CHUNK:2 * KEY_CHUNK, :] = pad

    def count16(ref, pred):
        def body(c, acc):
            m = jnp.where(pred(ref[c]), jnp.int16(1), jnp.int16(0))
            parts = [m[r:r + 2 * SUBLANES, :] for r in range(0, 2 * KEY_CHUNK, 2 * SUBLANES)]
            while len(parts) > 1:
                parts = [parts[a] + parts[a + 1] for a in range(0, len(parts), 2)]
            return acc + parts[0]
        acc = lax.fori_loop(0, npair, body, jnp.zeros((2 * SUBLANES, QB), jnp.int16))
        return jnp.sum(acc.astype(jnp.int32), axis=0, keepdims=True)

    def bisect16(ref, target):
        def step(it, thr):
            bit = jnp.where(it == 0, jnp.int32(-32768), lax.shift_left(jnp.int32(1), 15 - it))
            cand = thr ^ bit
            cand16 = cand.astype(jnp.int16)
            cnt = count16(ref, lambda kc: kc >= cand16)
            return jnp.where(cnt >= target, cand, thr)
        return lax.fori_loop(0, 16, step, jnp.full((1, QB), -32768, jnp.int32))

    thr_hi = bisect16(hi_ref, n_sel)
    thr_hi16 = thr_hi.astype(jnp.int16)
    need_lo = n_sel - count16(hi_ref, lambda kc: kc > thr_hi16)

    def bucket_body(c, carry):
        lo_ref[c] = jnp.where(hi_ref[c] == thr_hi16, lo_ref[c], jnp.int16(-32768))
        return carry

    lax.fori_loop(0, npair, bucket_body, 0)
    thr_lo = bisect16(lo_ref, need_lo)
    thr = lax.shift_left(thr_hi, 16) | (thr_lo + 32768)
    thr = jnp.maximum(thr, INT_MIN + 1)

    thr_lo16 = thr_lo.astype(jnp.int16)
    cnt_ge = (n_sel - need_lo) + count16(lo_ref, lambda kc: kc >= thr_lo16)
    overflow = jnp.max(jnp.where((cnt_ge > n_sel) | (thr_lo == -32768), 1, 0))

    @pl.when(overflow == 0)
    def _():
        def body(c, carry):
            bias_ref[c] = jnp.where(keys_ref[c] >= thr, sel_bias, MASK_BIAS)
            return carry
        lax.fori_loop(0, nk, body, 0)

    @pl.when(overflow != 0)
    def _():
        need = (need_lo - count16(lo_ref, lambda kc: kc > thr_lo16)).astype(F32)

        def rank_chunks(chunks, earlier):
            ones = [jnp.where(keys_ref[c] == thr, 1.0, 0.0) for c in chunks]
            ranks = [_dot(lower_ref[...], o.astype(BF16)) for o in ones]
            for c, o, rank in zip(chunks, ones, ranks):
                kc = keys_ref[c]
                tie_bias = jnp.where(rank + earlier < need, sel_bias, MASK_BIAS)
                bias_ref[c] = jnp.where(kc > thr, sel_bias, jnp.where(kc == thr, tie_bias, MASK_BIAS))
                earlier = earlier + jnp.sum(_fold_rows(o, jnp.add), axis=0, keepdims=True)
            return earlier

        earlier = lax.fori_loop(0, nk >> 1, lambda p, e: rank_chunks([2 * p, 2 * p + 1], e),
                                jnp.zeros((1, QB), F32))

        @pl.when((nk & 1) == 1)
        def _():
            rank_chunks([nk - 1], earlier)

    acc_ref[...] = jnp.zeros_like(acc_ref)

    def group_logits(c, g):
        bias = bias_ref[c]
        return _dot(k_ref[chunk_rows(c), :], qpT_ref[0, g]) + jnp.concatenate([bias] * A_REP, axis=1)

    def attend(shifts):
        def attn_chunks(chunks):
            xs = [[group_logits(c, g) for g in range(A_KV_HEADS)] for c in chunks]
            for c, xc in zip(chunks, xs):
                for g in range(A_KV_HEADS):
                    x = xc[g]
                    if shifts is not None:
                        x = x - shifts[g]
                    p = jnp.exp(x).astype(BF16)
                    acc_ref[g] += _dot(vaugT_ref[c, g * LANES:(g + 1) * LANES, :], p)

        def attn_pair(p, carry):
            attn_chunks([2 * p, 2 * p + 1])
            return carry
        lax.fori_loop(0, nk >> 1, attn_pair, 0)

        @pl.when((nk & 1) == 1)
        def _():
            attn_chunks([nk - 1])

    @pl.when(small_bound)
    def _():
        attend(None)

    @pl.when(jnp.logical_not(small_bound))
    def _():
        def max_chunk(c, maxes):
            return tuple(jnp.maximum(maxes[g], _fold_rows(group_logits(c, g), jnp.maximum))
                         for g in range(A_KV_HEADS))
        start = (jnp.full((SUBLANES, A_REP * QB), MASK_BIAS, F32),) * A_KV_HEADS
        maxes = lax.fori_loop(0, nk, max_chunk, start)
        attend([jnp.max(m, axis=0, keepdims=True) for m in maxes])

    for pr in range(A_HEADS // 2):
        parts = []
        for hh in (2 * pr, 2 * pr + 1):
            g, r = divmod(hh, A_REP)
            a = acc_ref[g, :, r * QB:(r + 1) * QB]
            lo, hi = a[0:HEAD_DIM], a[HEAD_DIM:LANES]
            parts.append(lo / hi if g == 0 else hi / lo)
        o_ref[:, pr * LANES:(pr + 1) * LANES] = jnp.concatenate(parts, axis=0).T.astype(BF16)


def _dsa_attention(logit_bound, qpT, k, vaugT, qipT, kiw, wrow, batch, seq):
    n = k.shape[0]
    nq = seq // QB
    nc = seq // KEY_CHUNK
    n_sel = min(TOPK_MAX, seq // 4)
    r = np.arange(KEY_CHUNK)
    lower = jnp.asarray(r[None, :] < r[:, None], BF16)
    full = lambda wd: pl.BlockSpec((seq, wd), lambda b, i: (b, 0))
    kern = functools.partial(_dsa_kernel, n_sel=n_sel)
    return pl.pallas_call(
        kern,
        grid=(batch, nq),
        in_specs=[pl.BlockSpec(memory_space=pltpu.SMEM),
                  pl.BlockSpec((1, A_KV_HEADS, LANES, A_REP * QB), lambda b, i: (b * nq + i, 0, 0, 0)),
                  pl.BlockSpec((1, LANES, IDX_HEADS * QB), lambda b, i: (b * nq + i, 0, 0)),
                  pl.BlockSpec((1, SUBLANES, IDX_HEADS * QB), lambda b, i: (b * nq + i, 0, 0)),
                  full(LANES), full(LANES), pl.BlockSpec((nc, 2 * LANES, KEY_CHUNK), lambda b, i: (b, 0, 0)),
                  _const_spec(lower.shape)],
        out_specs=pl.BlockSpec((QB, A_HEADS * HEAD_DIM), lambda b, i: (b * nq + i, 0)),
        out_shape=jax.ShapeDtypeStruct((n, A_HEADS * HEAD_DIM), BF16),
        scratch_shapes=[pltpu.VMEM((nc, KEY_CHUNK, QB), jnp.int32),
                        pltpu.VMEM((nc // 2, 2 * KEY_CHUNK, QB), jnp.int16),
                        pltpu.VMEM((nc // 2, 2 * KEY_CHUNK, QB), jnp.int16), pltpu.VMEM((nc, KEY_CHUNK, QB), F32),
                        pltpu.VMEM((A_KV_HEADS, LANES, A_REP * QB), F32)],
        compiler_params=_params(("parallel", "arbitrary")),
        name="dsa_attention",
    )(logit_bound, qpT, qipT, wrow, k, kiw, vaugT, lower)


def _odd_proj_kernel(x_ref, g_ref, wk_ref, wt_ref, qT_ref, k_ref, vT_ref):
    h = _rms_rows(x_ref[...], g_ref[...]).astype(BF16)
    tm = h.shape[0]
    width = wk_ref.shape[1]
    top_rows = lax.broadcasted_iota(jnp.int32, (LANES, tm), 0) < HEAD_DIM
    k_ref[...] = _dot(h, wk_ref[...]).astype(BF16)
    yT = _dot_nt(wt_ref[...], h)
    for pp in range(width // LANES):
        y = yT[pp * LANES:(pp + 1) * LANES, :] * (HEAD_DIM ** -0.5)
        first = jnp.where(top_rows, y, 0.0).astype(BF16)
        second = jnp.where(top_rows, 0.0, y).astype(BF16)
        for j in range(tm // QB):
            qT_ref[j, pp, :, 0:QB] = first[:, j * QB:(j + 1) * QB]
            qT_ref[j, pp, :, QB:2 * QB] = second[:, j * QB:(j + 1) * QB]
    vT = yT[width:2 * width, :].astype(BF16)
    for j in range(tm // KEY_CHUNK):
        vT_ref[j] = vT[:, j * KEY_CHUNK:(j + 1) * KEY_CHUNK]


def _odd_proj(x, g, wk, wt, tm):
    n, d = x.shape
    width = wk.shape[1]
    pairs = width // LANES
    return pl.pallas_call(
        _odd_proj_kernel,
        grid=(n // tm,),
        in_specs=[pl.BlockSpec((tm, d), lambda i: (i, 0)), _const_spec((1, d)), _const_spec(wk.shape),
                  _const_spec(wt.shape)],
        out_specs=[pl.BlockSpec((tm // QB, pairs, LANES, 2 * QB), lambda i: (i, 0, 0, 0)),
                   pl.BlockSpec((tm, width), lambda i: (i, 0)),
                   pl.BlockSpec((tm // KEY_CHUNK, width, KEY_CHUNK), lambda i: (i, 0, 0))],
        out_shape=[jax.ShapeDtypeStruct((n // QB, pairs, LANES, 2 * QB), BF16),
                   jax.ShapeDtypeStruct((n, width), BF16),
                   jax.ShapeDtypeStruct((n // KEY_CHUNK, width, KEY_CHUNK), BF16)],
        compiler_params=_params(("parallel",)),
        name="odd_proj",
    )(x, g, wk, wt)


def _sb_kernel(qT_ref, k_ref, vT_ref, tri_ref, o_ref, acc_ref, carry_ref):
    i = pl.program_id(2)
    q = qT_ref[0, 0]

    krel = lax.broadcasted_iota(jnp.int32, (KEY_CHUNK, 2 * QB), 0)
    qrel = lax.broadcasted_iota(jnp.int32, (KEY_CHUNK, 2 * QB), 1) & (QB - 1)
    strict = krel < qrel

    def masked(x, diagonal):
        if not diagonal:
            return x
        rows = x.shape[0] - KEY_CHUNK
        last = jnp.where(strict, x[rows:], 0.0)
        return last if rows == 0 else jnp.concatenate([x[:rows], last], axis=0)

    def block(segs, carry, diagonal):
        chunk0, off0, _ = segs[0]
        start = pl.multiple_of(chunk0 * KEY_CHUNK, KEY_CHUNK) + off0
        bounds = [sum(r for _, _, r in segs[:n]) for n in range(len(segs) + 1)]
        z = _dot(k_ref[pl.ds(start, bounds[-1]), :], q)
        l1p = jnp.log(1.0 + jnp.exp(-jnp.abs(z)))
        sp = jnp.maximum(z, 0.0) + l1p
        ls = z - sp
        sp = masked(sp, diagonal)
        sums = [jnp.sum(_fold_rows(sp[bounds[n]:bounds[n + 1]], jnp.add), axis=0, keepdims=True)
                for n in range(len(segs))]
        pv = None
        for n, (chunk, off, rows) in enumerate(segs):
            later = _dot(tri_ref[0:rows, 0:rows], sp[bounds[n]:bounds[n + 1]].astype(BF16)) + carry
            for nxt in range(n + 1, len(segs)):
                later = later + sums[nxt]
            a = jnp.exp(ls[bounds[n]:bounds[n + 1]] - later)
            if diagonal and n == len(segs) - 1:
                a = jnp.where(strict, a, 0.0)
            parts = [a.astype(BF16)]
            if off:
                parts.insert(0, jnp.zeros((off, 2 * QB), BF16))
            if off + rows < KEY_CHUNK:
                parts.append(jnp.zeros((KEY_CHUNK - off - rows, 2 * QB), BF16))
            term = _dot(vT_ref[chunk], parts[0] if len(parts) == 1 else jnp.concatenate(parts, axis=0))
            pv = term if pv is None else pv + term
        acc_ref[...] += pv
        for n in range(len(segs)):
            carry = carry + sums[n]
        return carry

    def unfinished(carry):
        return (jnp.min(carry) <= EXP_UNDERFLOW).astype(jnp.int32)

    acc_ref[...] = jnp.zeros_like(acc_ref)
    zero = jnp.zeros((1, 2 * QB), F32)
    carry_ref[...] = zero

    @pl.when(i == 0)
    def _():
        block([(0, 0, KEY_CHUNK)], zero, True)

    @pl.when(i > 0)
    def _():
        carry = block([(i - 1, KEY_CHUNK - SB_TAIL, SB_TAIL), (i, 0, KEY_CHUNK)], zero, True)
        carry_ref[...] = carry

        @pl.when(unfinished(carry) > 0)
        def _():
            carry_ref[...] = block([(i - 1, 0, KEY_CHUNK - SB_TAIL)], carry, False)

    def cond(st):
        return jnp.logical_and(st[0] >= 0, st[1] > 0)

    def body(st):
        carry = block([(st[0], 0, KEY_CHUNK)], st[2], False)
        return st[0] - 1, unfinished(carry), carry

    carry = carry_ref[...]
    lax.while_loop(cond, body, (i - 2, unfinished(carry), carry))
    oT = jnp.concatenate([acc_ref[0:HEAD_DIM, 0:QB], acc_ref[HEAD_DIM:LANES, QB:2 * QB]], axis=0)
    o_ref[...] = oT.T.astype(BF16)


def _sb_attention(qT, k, vT, batch, seq):
    n, width = k.shape
    pairs = width // LANES
    nq = seq // QB
    nc = seq // KEY_CHUNK
    r = np.arange(KEY_CHUNK)
    tri = jnp.asarray(r[None, :] > r[:, None], BF16)
    return pl.pallas_call(
        _sb_kernel,
        grid=(batch, pairs, nq),
        in_specs=[pl.BlockSpec((1, 1, LANES, 2 * QB), lambda b, p, i: (b * nq + i, p, 0, 0)),
                  pl.BlockSpec((seq, LANES), lambda b, p, i: (b, p)),
                  pl.BlockSpec((nc, LANES, KEY_CHUNK), lambda b, p, i: (b, p, 0)),
                  _const_spec(tri.shape)],
        out_specs=pl.BlockSpec((QB, LANES), lambda b, p, i: (b * nq + i, p)),
        out_shape=jax.ShapeDtypeStruct((n, width), BF16),
        scratch_shapes=[pltpu.VMEM((LANES, 2 * QB), F32), pltpu.VMEM((1, 2 * QB), F32)],
        compiler_params=_params(("parallel", "parallel", "arbitrary")),
        name="sb_attention",
    )(qT, k, vT, tri)


def _pooled(u_ref, halo_ref, pw_ref, ps_ref, ext_ref, seq, tm):
    tpos0 = (pl.program_id(0) * tm) % seq
    halo = jnp.where(tpos0 == 0, 0.0, halo_ref[...])
    ext_ref[0:POOL_HALO, :] = halo
    ext_ref[POOL_HALO:POOL_HALO + tm, :] = u_ref[...]
    t_idx = lax.broadcasted_iota(jnp.int32, (tm, POOL_GROUP), 0) + tpos0
    out = []
    for gi, w in enumerate(POOL_WINDOWS):
        sl = slice(gi * POOL_GROUP, (gi + 1) * POOL_GROUP)
        s = ext_ref[POOL_HALO:POOL_HALO + tm, sl]
        for dlt in range(1, w):
            s = s + ext_ref[POOL_HALO - dlt:POOL_HALO - dlt + tm, sl]
        cnt = jnp.minimum(t_idx + 1, w).astype(F32)
        mixed = s / cnt - u_ref[:, sl]
        y = _dot(mixed.astype(BF16), pw_ref[gi]) * ps_ref[:, sl]
        out.append(y.astype(BF16))
    return out


def _post_kernel(*refs, even, seq, tm):
    if even:
        (attn_ref, u_ref, halo_ref, pw_ref, ps_ref, wo_ref, x_ref, g_ref, w1_ref, w3_ref, w2_ref,
         o_ref, ext_ref) = refs
        half = attn_ref.shape[1]
        y = _dot(attn_ref[...], wo_ref[0:half, :])
        for gi, pg in enumerate(_pooled(u_ref, halo_ref, pw_ref, ps_ref, ext_ref, seq, tm)):
            y = y + _dot(pg, wo_ref[half + gi * POOL_GROUP:half + (gi + 1) * POOL_GROUP, :])
    else:
        attn_ref, wo_ref, x_ref, g_ref, w1_ref, w3_ref, w2_ref, o_ref = refs
        y = _dot(attn_ref[...], wo_ref[...])
    x1 = x_ref[...] + y
    h = _rms_rows(x1, g_ref[...]).astype(BF16)
    a = _dot(h, w1_ref[...])
    b = _dot(h, w3_ref[...])
    gate = (a * (1.0 / (1.0 + jnp.exp(-a))) * b).astype(BF16)
    o_ref[...] = x1 + _dot(gate, w2_ref[...])


def _post(x, attn, w_out, g, w1, w3, w2, tm, seq, pool=None):
    n, d = x.shape
    even = pool is not None
    row = lambda wd: pl.BlockSpec((tm, wd), lambda i: (i, 0))
    ffn_specs = [_const_spec(w_out.shape), row(d), _const_spec((1, d)),
                 _const_spec(w1.shape), _const_spec(w3.shape), _const_spec(w2.shape)]
    scratch = []
    if even:
        u, pool_w, pool_scale = pool
        halo_blocks = tm // POOL_HALO
        halo = pl.BlockSpec((POOL_HALO, u.shape[1]), lambda i: (jnp.maximum(i * halo_blocks - 1, 0), 0))
        in_specs = [row(attn.shape[1]), row(u.shape[1]), halo, _const_spec(pool_w.shape),
                    _const_spec(pool_scale.shape)] + ffn_specs
        args = (attn, u, u, pool_w, pool_scale, w_out, x, g, w1, w3, w2)
        scratch.append(pltpu.VMEM((POOL_HALO + tm, u.shape[1]), F32))
    else:
        in_specs = [row(attn.shape[1])] + ffn_specs
        args = (attn, w_out, x, g, w1, w3, w2)
    kern = functools.partial(_post_kernel, even=even, seq=seq, tm=tm)
    return pl.pallas_call(
        kern,
        grid=(n // tm,),
        in_specs=in_specs,
        out_specs=row(d),
        out_shape=jax.ShapeDtypeStruct((n, d), F32),
        scratch_shapes=scratch,
        compiler_params=_params(("parallel",)),
        name="post_even" if even else "post_odd",
    )(*args)


def _even_in_weights(w_in, q_g, k_g):
    d = w_in.shape[0]
    a_w, kv_w, qi_w = A_HEADS * HEAD_DIM, A_KV_HEADS * HEAD_DIM, IDX_HEADS * IDX_DIM
    o_k, o_v, o_qi = a_w, a_w + kv_w, a_w + 2 * kv_w
    o_ki = o_qi + qi_w
    o_wi = o_ki + IDX_DIM
    o_u = o_wi + IDX_HEADS
    idx_scale = (IDX_HEADS * IDX_DIM) ** -0.5
    w_idx = w_in[:, o_wi:o_u] * idx_scale

    def head_pad(w, heads, before, after):
        w = jnp.pad(w.reshape(d, heads, HEAD_DIM), ((0, 0), (0, 0), (before, after)))
        return w.reshape(d, heads * LANES)

    wq = w_in[:, :a_w]
    t_cols = [head_pad(wq[:, :a_w // 2], A_REP, 0, HEAD_DIM), head_pad(wq[:, a_w // 2:], A_REP, HEAD_DIM, 0),
              w_in[:, o_v:o_qi], head_pad(w_in[:, o_qi:o_ki], IDX_HEADS, 0, LANES - IDX_DIM),
              jnp.pad(w_idx, ((0, 0), (0, WT_ROWS - IDX_HEADS)))]
    wt = jnp.concatenate(t_cols, axis=1).T.astype(BF16)
    kiw = jnp.pad(jnp.concatenate([w_in[:, o_ki:o_wi], w_idx], axis=1),
                  ((0, 0), (0, LANES - IDX_DIM - IDX_HEADS)))
    wc = jnp.concatenate([w_in[:, o_k:o_v], kiw, w_in[:, o_u:]], axis=1).astype(BF16)
    gq = jnp.pad((q_g * (HEAD_DIM ** -0.5))[None, :], ((0, 0), (0, HEAD_DIM)))
    qg = jnp.concatenate([jnp.tile(gq, (A_REP, 1)), jnp.tile(jnp.roll(gq, HEAD_DIM, axis=1), (A_REP, 1))])
    qg_rows = jnp.broadcast_to(qg.reshape(QP_W)[:, None], (QP_W, LANES))
    return wc, wt, qg_rows, jnp.tile(k_g, A_KV_HEADS)[None, :]


def _odd_in_weights(w_qkv):
    width = w_qkv.shape[1] // 3
    wk = w_qkv[:, width:2 * width].astype(BF16)
    wt = jnp.concatenate([w_qkv[:, :width], w_qkv[:, 2 * width:]], axis=1).T.astype(BF16)
    return wk, wt


def _trunk(x, norm_mix_g, norm_ffn_g, ev_w_in, ev_w_out, ev_q_norm_g, ev_k_norm_g, ev_pool_w,
           ev_pool_scale, od_w_qkv, od_w_out, ffn_w1, ffn_w3, ffn_w2, *, tm):
    batch, seq, d = x.shape
    depth = norm_mix_g.shape[0]
    xf = x.reshape(batch * seq, d)
    for layer in range(depth):
        li = layer // 2
        g_mix = norm_mix_g[layer][None, :]
        g_ffn = norm_ffn_g[layer][None, :]
        w1, w3, w2 = (w[layer].astype(BF16) for w in (ffn_w1, ffn_w3, ffn_w2))
        if layer % 2 == 0:
            wc, wt, qg, kg = _even_in_weights(ev_w_in[li], ev_q_norm_g[li], ev_k_norm_g[li])
            qpT, k, vaugT, qipT, kiw, wrow, u = _even_proj(xf, g_mix, wc, wt, qg, kg, tm)
            bound = (HEAD_DIM ** 0.5) * jnp.max(jnp.abs(ev_q_norm_g[li])) * jnp.max(jnp.abs(ev_k_norm_g[li]))
            bound = (ROUNDING_SLACK * bound).astype(F32).reshape(1)
            attn = _dsa_attention(bound, qpT, k, vaugT, qipT, kiw, wrow, batch, seq)
            pool = (u, ev_pool_w[li].astype(BF16), ev_pool_scale[li][None, :])
            xf = _post(xf, attn, ev_w_out[li].astype(BF16), g_ffn, w1, w3, w2, tm, seq, pool=pool)
        else:
            wk, wt = _odd_in_weights(od_w_qkv[li])
            qT, k, vT = _odd_proj(xf, g_mix, wk, wt, tm)
            attn = _sb_attention(qT, k, vT, batch, seq)
            xf = _post(xf, attn, od_w_out[li].astype(BF16), g_ffn, w1, w3, w2, tm, seq)
    return xf.reshape(batch, seq, d)


def kernel(x, norm_mix_g, norm_ffn_g, ev_w_in, ev_w_out, ev_q_norm_g, ev_k_norm_g, ev_pool_w,
           ev_pool_scale, od_w_qkv, od_w_out, ffn_w1, ffn_w3, ffn_w2):
    return _trunk(x, norm_mix_g, norm_ffn_g, ev_w_in, ev_w_out, ev_q_norm_g, ev_k_norm_g, ev_pool_w,
                  ev_pool_scale, od_w_qkv, od_w_out, ffn_w1, ffn_w3, ffn_w2, tm=512)
```

```python
import functools

import jax
import jax.numpy as jnp
import numpy as np
from jax import lax
from jax.experimental import pallas as pl
from jax.experimental.pallas import tpu as pltpu

F32 = jnp.float32
BF16 = jnp.bfloat16

HEAD_DIM = 64
A_HEADS = 8
A_KV_HEADS = 2
A_REP = A_HEADS // A_KV_HEADS
IDX_HEADS = 4
IDX_DIM = 64
TOPK_MAX = 256
POOL_WINDOWS = (2, 4, 8, 16)
POOL_GROUP = 128
POOL_WIDTH = POOL_GROUP * len(POOL_WINDOWS)
POOL_HALO = 16
EPS = 1e-6

LANES = 128
SUBLANES = 8
KEY_CHUNK = 256
INT_MIN = np.int32(-(2**31))
MASK_BIAS = -1e30
MAX_CONST_SHIFT = 20.0
ROUNDING_SLACK = 1.01
EXP_UNDERFLOW = 105.0
SB_TAIL = 192
VMEM_LIMIT = 56 * 1024 * 1024

_NT = (((1,), (1,)), ((), ()))


def _dot(a, b):
    return jnp.dot(a, b, preferred_element_type=F32)


def _dot_nt(a, b):
    return lax.dot_general(a, b, _NT, preferred_element_type=F32)


def _split_bf16(x):
    hi = x.astype(BF16)
    lo = (x - hi.astype(F32)).astype(BF16)
    return hi, lo


def _rms_rows(x, g):
    ms = jnp.mean(x * x, axis=-1, keepdims=True)
    return x * lax.rsqrt(ms + EPS) * g


def _fold_rows(x, op):
    parts = [x[r:r + SUBLANES, :] for r in range(0, x.shape[0], SUBLANES)]
    while len(parts) > 1:
        nxt = [op(parts[a], parts[a + 1]) for a in range(0, len(parts) - 1, 2)]
        if len(parts) % 2:
            nxt.append(parts[-1])
        parts = nxt
    return parts[0]


def _const_spec(shape):
    nd = len(shape)
    return pl.BlockSpec(shape, lambda *_: (0,) * nd, pipeline_mode=pl.Buffered(1))


def _params(sem):
    return pltpu.CompilerParams(dimension_semantics=sem, vmem_limit_bytes=VMEM_LIMIT)


QB = KEY_CHUNK
QP_W = A_HEADS * LANES
QIP_W = IDX_HEADS * LANES
WT_ROWS = 16
_ROW_V = QP_W
_ROW_QI = _ROW_V + LANES
_ROW_W = _ROW_QI + QIP_W
EVEN_T_ROWS = _ROW_W + WT_ROWS
_COL_KIW = LANES
_COL_U = 2 * LANES
EVEN_COLS = _COL_U + POOL_WIDTH


def _even_proj_kernel(x_ref, g_ref, wc_ref, wt_ref, qg_ref, kg_ref, seg_ref,
                      qpT_ref, k_ref, vaugT_ref, qipT_ref, kiw_ref, wrow_ref, u_ref):
    h = _rms_rows(x_ref[...], g_ref[...]).astype(BF16)
    tm = h.shape[0]
    reps = tm // LANES
    blocks = tm // QB
    yT = _dot_nt(wt_ref[...], h)

    for hh in range(A_HEADS):
        rows = slice(hh * LANES, (hh + 1) * LANES)
        y = yT[rows, :]
        ss = jnp.sum(y * y, axis=0, keepdims=True)
        gain = jnp.concatenate([qg_ref[rows, :]] * reps, axis=1)
        qT = (y * lax.rsqrt(ss * (1.0 / HEAD_DIM) + EPS) * gain).astype(BF16)
        g, r = divmod(hh, A_REP)
        for j in range(blocks):
            qpT_ref[j, g, :, r * QB:(r + 1) * QB] = qT[:, j * QB:(j + 1) * QB]

    yc = _dot(h, wc_ref[...])
    y = yc[:, 0:LANES]
    hi, lo = _split_bf16(y * y)
    ss = _dot(hi, seg_ref[...]) + _dot(lo, seg_ref[...])
    k_ref[...] = (y * lax.rsqrt(ss * (1.0 / HEAD_DIM) + EPS) * kg_ref[...]).astype(BF16)

    vT = yT[_ROW_V:_ROW_V + LANES, :]
    one = jnp.ones((HEAD_DIM, tm), F32)
    aug = jnp.concatenate([vT[0:HEAD_DIM], one, one, vT[HEAD_DIM:LANES]], axis=0).astype(BF16)
    for j in range(tm // KEY_CHUNK):
        vaugT_ref[j] = aug[:, j * KEY_CHUNK:(j + 1) * KEY_CHUNK]

    for hh in range(IDX_HEADS):
        qiT = yT[_ROW_QI + hh * LANES:_ROW_QI + (hh + 1) * LANES, :].astype(BF16)
        for j in range(blocks):
            qipT_ref[j, :, hh * QB:(hh + 1) * QB] = qiT[:, j * QB:(j + 1) * QB]
            wrow_ref[j, :, hh * QB:(hh + 1) * QB] = jnp.broadcast_to(
                yT[_ROW_W + hh:_ROW_W + hh + 1, j * QB:(j + 1) * QB], (SUBLANES, QB))
    kiw_ref[...] = yc[:, _COL_KIW:_COL_KIW + LANES].astype(BF16)
    u_ref[...] = yc[:, _COL_U:_COL_U + POOL_WIDTH]


def _even_proj(x, g, wc, wt, qg, kg, tm):
    n, d = x.shape
    r = np.arange(LANES) // HEAD_DIM
    seg = jnp.asarray(r[:, None] == r[None, :], BF16)
    row = lambda wd: pl.BlockSpec((tm, wd), lambda i: (i, 0))
    cpt = tm // KEY_CHUNK
    bpt = tm // QB
    out_specs = [pl.BlockSpec((bpt, A_KV_HEADS, LANES, A_REP * QB), lambda i: (i, 0, 0, 0)), row(LANES),
                 pl.BlockSpec((cpt, 2 * LANES, KEY_CHUNK), lambda i: (i, 0, 0)),
                 pl.BlockSpec((bpt, LANES, IDX_HEADS * QB), lambda i: (i, 0, 0)), row(LANES),
                 pl.BlockSpec((bpt, SUBLANES, IDX_HEADS * QB), lambda i: (i, 0, 0)), row(POOL_WIDTH)]
    out_shape = [jax.ShapeDtypeStruct((n // QB, A_KV_HEADS, LANES, A_REP * QB), BF16),
                 jax.ShapeDtypeStruct((n, LANES), BF16),
                 jax.ShapeDtypeStruct((n // KEY_CHUNK, 2 * LANES, KEY_CHUNK), BF16),
                 jax.ShapeDtypeStruct((n // QB, LANES, IDX_HEADS * QB), BF16),
                 jax.ShapeDtypeStruct((n, LANES), BF16),
                 jax.ShapeDtypeStruct((n // QB, SUBLANES, IDX_HEADS * QB), F32),
                 jax.ShapeDtypeStruct((n, POOL_WIDTH), F32)]
    return pl.pallas_call(
        _even_proj_kernel,
        grid=(n // tm,),
        in_specs=[row(d), _const_spec((1, d)), _const_spec(wc.shape), _const_spec(wt.shape),
                  _const_spec(qg.shape), _const_spec((1, LANES)), _const_spec((LANES, LANES))],
        out_specs=out_specs,
        out_shape=out_shape,
        compiler_params=_params(("parallel",)),
        name="even_proj",
    )(x, g, wc, wt, qg, kg, seg)


def _dsa_kernel(shift_ref, qpT_ref, qipT_ref, wrow_ref, k_ref, kiw_ref, vaugT_ref, lower_ref, o_ref,
                keys_ref, hi_ref, lo_ref, bias_ref, acc_ref, *, n_sel):
    i = pl.program_id(1)
    nk = i + 1
    kpos = lax.broadcasted_iota(jnp.int32, (KEY_CHUNK, QB), 0)
    qpos = lax.broadcasted_iota(jnp.int32, (KEY_CHUNK, QB), 1)
    causal = kpos <= qpos
    logit_bound = shift_ref[0]
    small_bound = logit_bound <= MAX_CONST_SHIFT
    sel_bias = jnp.where(small_bound, -logit_bound, 0.0)

    def chunk_rows(c):
        return pl.ds(pl.multiple_of(c * KEY_CHUNK, KEY_CHUNK), KEY_CHUNK)

    def per_head(x, n):
        return [x[:, r * QB:(r + 1) * QB] for r in range(n)]

    def score_matmul(c):
        return _dot(kiw_ref[chunk_rows(c), :], qipT_ref[0])

    def score_chunk(c, diagonal, d=None):
        d = score_matmul(c) if d is None else d
        terms = per_head(jnp.maximum(d, 0.0) * wrow_ref[0, 0:1, :], IDX_HEADS)
        s = (terms[0] + terms[1]) + (terms[2] + terms[3])
        bits = lax.bitcast_convert_type(s, jnp.int32)
        key = bits ^ ((bits >> 31) & jnp.int32(0x7FFFFFFF))
        if diagonal:
            key = jnp.where(causal, key, INT_MIN)
        keys_ref[c] = key
        half = pl.ds(pl.multiple_of((c & 1) * KEY_CHUNK, KEY_CHUNK), KEY_CHUNK)
        hi_ref[c >> 1, half, :] = (key >> 16).astype(jnp.int16)
        lo_ref[c >> 1, half, :] = ((key & 0xFFFF) - 32768).astype(jnp.int16)

    def score_pair(p, carry):
        ds = [score_matmul(2 * p), score_matmul(2 * p + 1)]
        score_chunk(2 * p, False, ds[0])
        score_chunk(2 * p + 1, False, ds[1])
        return carry

    lax.fori_loop(0, i >> 1, score_pair, 0)

    @pl.when((i & 1) == 1)
    def _():
        score_chunk(i - 1, False)

    score_chunk(i, True)

    npair = (nk + 1) >> 1

    @pl.when((nk & 1) == 1)
    def _():
        pad = jnp.full((KEY_CHUNK, QB), -32768, jnp.int16)
        hi_ref[nk >> 1, KEY_CHUNK:2 * KEY_CHUNK, :] = pad
        lo_ref[nk >> 1, KEY_CHUNK:2 * KEY_CHUNK, :] = pad

    def count16(ref, pred):
        def body(c, acc):
            m = jnp.where(pred(ref[c]), jnp.int16(1), jnp.int16(0))
            parts = [m[r:r + 2 * SUBLANES, :] for r in range(0, 2 * KEY_CHUNK, 2 * SUBLANES)]
            while len(parts) > 1:
                parts = [parts[a] + parts[a + 1] for a in range(0, len(parts), 2)]
            return acc + parts[0]
        acc = lax.fori_loop(0, npair, body, jnp.zeros((2 * SUBLANES, QB), jnp.int16))
        return jnp.sum(acc.astype(jnp.int32), axis=0, keepdims=True)

    def bisect16(ref, target):
        def step(it, thr):
            bit = jnp.where(it == 0, jnp.int32(-32768), lax.shift_left(jnp.int32(1), 15 - it))
            cand = thr ^ bit
            cand16 = cand.astype(jnp.int16)
            cnt = count16(ref, lambda kc: kc >= cand16)
            return jnp.where(cnt >= target, cand, thr)
        return lax.fori_loop(0, 16, step, jnp.full((1, QB), -32768, jnp.int32))

    thr_hi = bisect16(hi_ref, n_sel)
    thr_hi16 = thr_hi.astype(jnp.int16)
    need_lo = n_sel - count16(hi_ref, lambda kc: kc > thr_hi16)

    def bucket_body(c, carry):
        lo_ref[c] = jnp.where(hi_ref[c] == thr_hi16, lo_ref[c], jnp.int16(-32768))
        return carry

    lax.fori_loop(0, npair, bucket_body, 0)
    thr_lo = bisect16(lo_ref, need_lo)
    thr = lax.shift_left(thr_hi, 16) | (thr_lo + 32768)
    thr = jnp.maximum(thr, INT_MIN + 1)

    thr_lo16 = thr_lo.astype(jnp.int16)
    cnt_ge = (n_sel - need_lo) + count16(lo_ref, lambda kc: kc >= thr_lo16)
    overflow = jnp.max(jnp.where((cnt_ge > n_sel) | (thr_lo == -32768), 1, 0))

    @pl.when(overflow == 0)
    def _():
        def body(c, carry):
            bias_ref[c] = jnp.where(keys_ref[c] >= thr, sel_bias, MASK_BIAS)
            return carry
        lax.fori_loop(0, nk, body, 0)

    @pl.when(overflow != 0)
    def _():
        need = (need_lo - count16(lo_ref, lambda kc: kc > thr_lo16)).astype(F32)

        def rank_chunks(chunks, earlier):
            ones = [jnp.where(keys_ref[c] == thr, 1.0, 0.0) for c in chunks]
            ranks = [_dot(lower_ref[...], o.astype(BF16)) for o in ones]
            for c, o, rank in zip(chunks, ones, ranks):
                kc = keys_ref[c]
                tie_bias = jnp.where(rank + earlier < need, sel_bias, MASK_BIAS)
                bias_ref[c] = jnp.where(kc > thr, sel_bias, jnp.where(kc == thr, tie_bias, MASK_BIAS))
                earlier = earlier + jnp.sum(_fold_rows(o, jnp.add), axis=0, keepdims=True)
            return earlier

        earlier = lax.fori_loop(0, nk >> 1, lambda p, e: rank_chunks([2 * p, 2 * p + 1], e),
                                jnp.zeros((1, QB), F32))

        @pl.when((nk & 1) == 1)
        def _():
            rank_chunks([nk - 1], earlier)

    acc_ref[...] = jnp.zeros_like(acc_ref)

    def group_logits(c, g):
        bias = bias_ref[c]
        return _dot(k_ref[chunk_rows(c), :], qpT_ref[0, g]) + jnp.concatenate([bias] * A_REP, axis=1)

    def attend(shifts):
        def attn_chunks(chunks):
            xs = [[group_logits(c, g) for g in range(A_KV_HEADS)] for c in chunks]
            for c, xc in zip(chunks, xs):
                for g in range(A_KV_HEADS):
                    x = xc[g]
                    if shifts is not None:
                        x = x - shifts[g]
                    p = jnp.exp(x).astype(BF16)
                    acc_ref[g] += _dot(vaugT_ref[c, g * LANES:(g + 1) * LANES, :], p)

        def attn_quad(p, carry):
            attn_chunks([4 * p, 4 * p + 1, 4 * p + 2, 4 * p + 3])
            return carry
        lax.fori_loop(0, nk >> 2, attn_quad, 0)

        @pl.when((nk & 2) == 2)
        def _():
            base = (nk >> 2) << 2
            attn_chunks([base, base + 1])

        @pl.when((nk & 1) == 1)
        def _():
            attn_chunks([nk - 1])

    @pl.when(small_bound)
    def _():
        attend(None)

    @pl.when(jnp.logical_not(small_bound))
    def _():
        def max_chunk(c, maxes):
            return tuple(jnp.maximum(maxes[g], _fold_rows(group_logits(c, g), jnp.maximum))
                         for g in range(A_KV_HEADS))
        start = (jnp.full((SUBLANES, A_REP * QB), MASK_BIAS, F32),) * A_KV_HEADS
        maxes = lax.fori_loop(0, nk, max_chunk, start)
        attend([jnp.max(m, axis=0, keepdims=True) for m in maxes])

    for pr in range(A_HEADS // 2):
        parts = []
        for hh in (2 * pr, 2 * pr + 1):
            g, r = divmod(hh, A_REP)
            a = acc_ref[g, :, r * QB:(r + 1) * QB]
            lo, hi = a[0:HEAD_DIM], a[HEAD_DIM:LANES]
            parts.append(lo / hi if g == 0 else hi / lo)
        o_ref[:, pr * LANES:(pr + 1) * LANES] = jnp.concatenate(parts, axis=0).T.astype(BF16)


def _dsa_attention(logit_bound, qpT, k, vaugT, qipT, kiw, wrow, batch, seq):
    n = k.shape[0]
    nq = seq // QB
    nc = seq // KEY_CHUNK
    n_sel = min(TOPK_MAX, seq // 4)
    r = np.arange(KEY_CHUNK)
    lower = jnp.asarray(r[None, :] < r[:, None], BF16)
    full = lambda wd: pl.BlockSpec((seq, wd), lambda b, i: (b, 0))
    kern = functools.partial(_dsa_kernel, n_sel=n_sel)
    return pl.pallas_call(
        kern,
        grid=(batch, nq),
        in_specs=[pl.BlockSpec(memory_space=pltpu.SMEM),
                  pl.BlockSpec((1, A_KV_HEADS, LANES, A_REP * QB), lambda b, i: (b * nq + i, 0, 0, 0)),
                  pl.BlockSpec((1, LANES, IDX_HEADS * QB), lambda b, i: (b * nq + i, 0, 0)),
                  pl.BlockSpec((1, SUBLANES, IDX_HEADS * QB), lambda b, i: (b * nq + i, 0, 0)),
                  full(LANES), full(LANES), pl.BlockSpec((nc, 2 * LANES, KEY_CHUNK), lambda b, i: (b, 0, 0)),
                  _const_spec(lower.shape)],
        out_specs=pl.BlockSpec((QB, A_HEADS * HEAD_DIM), lambda b, i: (b * nq + i, 0)),
        out_shape=jax.ShapeDtypeStruct((n, A_HEADS * HEAD_DIM), BF16),
        scratch_shapes=[pltpu.VMEM((nc, KEY_CHUNK, QB), jnp.int32),
                        pltpu.VMEM((nc // 2, 2 * KEY_CHUNK, QB), jnp.int16),
                        pltpu.VMEM((nc // 2, 2 * KEY_CHUNK, QB), jnp.int16), pltpu.VMEM((nc, KEY_CHUNK, QB), F32),
                        pltpu.VMEM((A_KV_HEADS, LANES, A_REP * QB), F32)],
        compiler_params=_params(("parallel", "arbitrary")),
        name="dsa_attention",
    )(logit_bound, qpT, qipT, wrow, k, kiw, vaugT, lower)


def _odd_proj_kernel(x_ref, g_ref, wk_ref, wt_ref, qT_ref, k_ref, vT_ref):
    h = _rms_rows(x_ref[...], g_ref[...]).astype(BF16)
    tm = h.shape[0]
    width = wk_ref.shape[1]
    top_rows = lax.broadcasted_iota(jnp.int32, (LANES, tm), 0) < HEAD_DIM
    k_ref[...] = _dot(h, wk_ref[...]).astype(BF16)
    yT = _dot_nt(wt_ref[...], h)
    for pp in range(width // LANES):
        y = yT[pp * LANES:(pp + 1) * LANES, :] * (HEAD_DIM ** -0.5)
        first = jnp.where(top_rows, y, 0.0).astype(BF16)
        second = jnp.where(top_rows, 0.0, y).astype(BF16)
        for j in range(tm // QB):
            qT_ref[j, pp, :, 0:QB] = first[:, j * QB:(j + 1) * QB]
            qT_ref[j, pp, :, QB:2 * QB] = second[:, j * QB:(j + 1) * QB]
    vT = yT[width:2 * width, :].astype(BF16)
    for j in range(tm // KEY_CHUNK):
        vT_ref[j] = vT[:, j * KEY_CHUNK:(j + 1) * KEY_CHUNK]


def _odd_proj(x, g, wk, wt, tm):
    n, d = x.shape
    width = wk.shape[1]
    pairs = width // LANES
    return pl.pallas_call(
        _odd_proj_kernel,
        grid=(n // tm,),
        in_specs=[pl.BlockSpec((tm, d), lambda i: (i, 0)), _const_spec((1, d)), _const_spec(wk.shape),
                  _const_spec(wt.shape)],
        out_specs=[pl.BlockSpec((tm // QB, pairs, LANES, 2 * QB), lambda i: (i, 0, 0, 0)),
                   pl.BlockSpec((tm, width), lambda i: (i, 0)),
                   pl.BlockSpec((tm // KEY_CHUNK, width, KEY_CHUNK), lambda i: (i, 0, 0))],
        out_shape=[jax.ShapeDtypeStruct((n // QB, pairs, LANES, 2 * QB), BF16),
                   jax.ShapeDtypeStruct((n, width), BF16),
                   jax.ShapeDtypeStruct((n // KEY_CHUNK, width, KEY_CHUNK), BF16)],
        compiler_params=_params(("parallel",)),
        name="odd_proj",
    )(x, g, wk, wt)


def _sb_kernel(qT_ref, k_ref, vT_ref, tri_ref, o_ref, acc_ref, carry_ref):
    i = pl.program_id(2)
    q = qT_ref[0, 0]

    krel = lax.broadcasted_iota(jnp.int32, (KEY_CHUNK, 2 * QB), 0)
    qrel = lax.broadcasted_iota(jnp.int32, (KEY_CHUNK, 2 * QB), 1) & (QB - 1)
    strict = krel < qrel

    def masked(x, diagonal):
        if not diagonal:
            return x
        rows = x.shape[0] - KEY_CHUNK
        last = jnp.where(strict, x[rows:], 0.0)
        return last if rows == 0 else jnp.concatenate([x[:rows], last], axis=0)

    def block(segs, carry, diagonal):
        chunk0, off0, _ = segs[0]
        start = pl.multiple_of(chunk0 * KEY_CHUNK, KEY_CHUNK) + off0
        bounds = [sum(r for _, _, r in segs[:n]) for n in range(len(segs) + 1)]
        z = _dot(k_ref[pl.ds(start, bounds[-1]), :], q)
        l1p = jnp.log(1.0 + jnp.exp(-jnp.abs(z)))
        sp = jnp.maximum(z, 0.0) + l1p
        ls = z - sp
        sp = masked(sp, diagonal)
        sums = [jnp.sum(_fold_rows(sp[bounds[n]:bounds[n + 1]], jnp.add), axis=0, keepdims=True)
                for n in range(len(segs))]
        pv = None
        for n, (chunk, off, rows) in enumerate(segs):
            later = _dot(tri_ref[0:rows, 0:rows], sp[bounds[n]:bounds[n + 1]].astype(BF16)) + carry
            for nxt in range(n + 1, len(segs)):
                later = later + sums[nxt]
            a = jnp.exp(ls[bounds[n]:bounds[n + 1]] - later)
            if diagonal and n == len(segs) - 1:
                a = jnp.where(strict, a, 0.0)
            parts = [a.astype(BF16)]
            if off:
                parts.insert(0, jnp.zeros((off, 2 * QB), BF16))
            if off + rows < KEY_CHUNK:
                parts.append(jnp.zeros((KEY_CHUNK - off - rows, 2 * QB), BF16))
            term = _dot(vT_ref[chunk], parts[0] if len(parts) == 1 else jnp.concatenate(parts, axis=0))
            pv = term if pv is None else pv + term
        acc_ref[...] += pv
        for n in range(len(segs)):
            carry = carry + sums[n]
        return carry

    def unfinished(carry):
        return (jnp.min(carry) <= EXP_UNDERFLOW).astype(jnp.int32)

    acc_ref[...] = jnp.zeros_like(acc_ref)
    zero = jnp.zeros((1, 2 * QB), F32)
    carry_ref[...] = zero

    @pl.when(i == 0)
    def _():
        block([(0, 0, KEY_CHUNK)], zero, True)

    @pl.when(i > 0)
    def _():
        carry = block([(i - 1, KEY_CHUNK - SB_TAIL, SB_TAIL), (i, 0, KEY_CHUNK)], zero, True)
        carry_ref[...] = carry

        @pl.when(unfinished(carry) > 0)
        def _():
            carry_ref[...] = block([(i - 1, 0, KEY_CHUNK - SB_TAIL)], carry, False)

    def cond(st):
        return jnp.logical_and(st[0] >= 0, st[1] > 0)

    def body(st):
        carry = block([(st[0], 0, KEY_CHUNK)], st[2], False)
        return st[0] - 1, unfinished(carry), carry

    carry = carry_ref[...]
    lax.while_loop(cond, body, (i - 2, unfinished(carry), carry))
    oT = jnp.concatenate([acc_ref[0:HEAD_DIM, 0:QB], acc_ref[HEAD_DIM:LANES, QB:2 * QB]], axis=0)
    o_ref[...] = oT.T.astype(BF16)


def _sb_attention(qT, k, vT, batch, seq):
    n, width = k.shape
    pairs = width // LANES
    nq = seq // QB
    nc = seq // KEY_CHUNK
    r = np.arange(KEY_CHUNK)
    tri = jnp.asarray(r[None, :] > r[:, None], BF16)
    return pl.pallas_call(
        _sb_kernel,
        grid=(batch, pairs, nq),
        in_specs=[pl.BlockSpec((1, 1, LANES, 2 * QB), lambda b, p, i: (b * nq + i, p, 0, 0)),
                  pl.BlockSpec((seq, LANES), lambda b, p, i: (b, p)),
                  pl.BlockSpec((nc, LANES, KEY_CHUNK), lambda b, p, i: (b, p, 0)),
                  _const_spec(tri.shape)],
        out_specs=pl.BlockSpec((QB, LANES), lambda b, p, i: (b * nq + i, p)),
        out_shape=jax.ShapeDtypeStruct((n, width), BF16),
        scratch_shapes=[pltpu.VMEM((LANES, 2 * QB), F32), pltpu.VMEM((1, 2 * QB), F32)],
        compiler_params=_params(("parallel", "parallel", "arbitrary")),
        name="sb_attention",
    )(qT, k, vT, tri)


def _pooled(u_ref, halo_ref, pw_ref, ps_ref, ext_ref, seq, tm):
    tpos0 = (pl.program_id(0) * tm) % seq
    halo = jnp.where(tpos0 == 0, 0.0, halo_ref[...])
    ext_ref[0:POOL_HALO, :] = halo
    ext_ref[POOL_HALO:POOL_HALO + tm, :] = u_ref[...]
    t_idx = lax.broadcasted_iota(jnp.int32, (tm, POOL_GROUP), 0) + tpos0
    out = []
    for gi, w in enumerate(POOL_WINDOWS):
        sl = slice(gi * POOL_GROUP, (gi + 1) * POOL_GROUP)
        s = ext_ref[POOL_HALO:POOL_HALO + tm, sl]
        for dlt in range(1, w):
            s = s + ext_ref[POOL_HALO - dlt:POOL_HALO - dlt + tm, sl]
        cnt = jnp.minimum(t_idx + 1, w).astype(F32)
        mixed = s / cnt - u_ref[:, sl]
        y = _dot(mixed.astype(BF16), pw_ref[gi]) * ps_ref[:, sl]
        out.append(y.astype(BF16))
    return out


def _post_kernel(*refs, even, seq, tm):
    if even:
        (attn_ref, u_ref, halo_ref, pw_ref, ps_ref, wo_ref, x_ref, g_ref, w1_ref, w3_ref, w2_ref,
         o_ref, ext_ref) = refs
        half = attn_ref.shape[1]
        y = _dot(attn_ref[...], wo_ref[0:half, :])
        for gi, pg in enumerate(_pooled(u_ref, halo_ref, pw_ref, ps_ref, ext_ref, seq, tm)):
            y = y + _dot(pg, wo_ref[half + gi * POOL_GROUP:half + (gi + 1) * POOL_GROUP, :])
    else:
        attn_ref, wo_ref, x_ref, g_ref, w1_ref, w3_ref, w2_ref, o_ref = refs
        y = _dot(attn_ref[...], wo_ref[...])
    x1 = x_ref[...] + y
    h = _rms_rows(x1, g_ref[...]).astype(BF16)
    a = _dot(h, w1_ref[...])
    b = _dot(h, w3_ref[...])
    gate = (a * (1.0 / (1.0 + jnp.exp(-a))) * b).astype(BF16)
    o_ref[...] = x1 + _dot(gate, w2_ref[...])


def _post(x, attn, w_out, g, w1, w3, w2, tm, seq, pool=None):
    n, d = x.shape
    even = pool is not None
    row = lambda wd: pl.BlockSpec((tm, wd), lambda i: (i, 0))
    ffn_specs = [_const_spec(w_out.shape), row(d), _const_spec((1, d)),
                 _const_spec(w1.shape), _const_spec(w3.shape), _const_spec(w2.shape)]
    scratch = []
    if even:
        u, pool_w, pool_scale = pool
        halo_blocks = tm // POOL_HALO
        halo = pl.BlockSpec((POOL_HALO, u.shape[1]), lambda i: (jnp.maximum(i * halo_blocks - 1, 0), 0))
        in_specs = [row(attn.shape[1]), row(u.shape[1]), halo, _const_spec(pool_w.shape),
                    _const_spec(pool_scale.shape)] + ffn_specs
        args = (attn, u, u, pool_w, pool_scale, w_out, x, g, w1, w3, w2)
        scratch.append(pltpu.VMEM((POOL_HALO + tm, u.shape[1]), F32))
    else:
        in_specs = [row(attn.shape[1])] + ffn_specs
        args = (attn, w_out, x, g, w1, w3, w2)
    kern = functools.partial(_post_kernel, even=even, seq=seq, tm=tm)
    return pl.pallas_call(
        kern,
        grid=(n // tm,),
        in_specs=in_specs,
        out_specs=row(d),
        out_shape=jax.ShapeDtypeStruct((n, d), F32),
        scratch_shapes=scratch,
        compiler_params=_params(("parallel",)),
        name="post_even" if even else "post_odd",
    )(*args)


def _even_in_weights(w_in, q_g, k_g):
    d = w_in.shape[0]
    a_w, kv_w, qi_w = A_HEADS * HEAD_DIM, A_KV_HEADS * HEAD_DIM, IDX_HEADS * IDX_DIM
    o_k, o_v, o_qi = a_w, a_w + kv_w, a_w + 2 * kv_w
    o_ki = o_qi + qi_w
    o_wi = o_ki + IDX_DIM
    o_u = o_wi + IDX_HEADS
    idx_scale = (IDX_HEADS * IDX_DIM) ** -0.5
    w_idx = w_in[:, o_wi:o_u] * idx_scale

    def head_pad(w, heads, before, after):
        w = jnp.pad(w.reshape(d, heads, HEAD_DIM), ((0, 0), (0, 0), (before, after)))
        return w.reshape(d, heads * LANES)

    wq = w_in[:, :a_w]
    t_cols = [head_pad(wq[:, :a_w // 2], A_REP, 0, HEAD_DIM), head_pad(wq[:, a_w // 2:], A_REP, HEAD_DIM, 0),
              w_in[:, o_v:o_qi], head_pad(w_in[:, o_qi:o_ki], IDX_HEADS, 0, LANES - IDX_DIM),
              jnp.pad(w_idx, ((0, 0), (0, WT_ROWS - IDX_HEADS)))]
    wt = jnp.concatenate(t_cols, axis=1).T.astype(BF16)
    kiw = jnp.pad(jnp.concatenate([w_in[:, o_ki:o_wi], w_idx], axis=1),
                  ((0, 0), (0, LANES - IDX_DIM - IDX_HEADS)))
    wc = jnp.concatenate([w_in[:, o_k:o_v], kiw, w_in[:, o_u:]], axis=1).astype(BF16)
    gq = jnp.pad((q_g * (HEAD_DIM ** -0.5))[None, :], ((0, 0), (0, HEAD_DIM)))
    qg = jnp.concatenate([jnp.tile(gq, (A_REP, 1)), jnp.tile(jnp.roll(gq, HEAD_DIM, axis=1), (A_REP, 1))])
    qg_rows = jnp.broadcast_to(qg.reshape(QP_W)[:, None], (QP_W, LANES))
    return wc, wt, qg_rows, jnp.tile(k_g, A_KV_HEADS)[None, :]


def _odd_in_weights(w_qkv):
    width = w_qkv.shape[1] // 3
    wk = w_qkv[:, width:2 * width].astype(BF16)
    wt = jnp.concatenate([w_qkv[:, :width], w_qkv[:, 2 * width:]], axis=1).T.astype(BF16)
    return wk, wt


def _trunk(x, norm_mix_g, norm_ffn_g, ev_w_in, ev_w_out, ev_q_norm_g, ev_k_norm_g, ev_pool_w,
           ev_pool_scale, od_w_qkv, od_w_out, ffn_w1, ffn_w3, ffn_w2, *, tm):
    batch, seq, d = x.shape
    depth = norm_mix_g.shape[0]
    xf = x.reshape(batch * seq, d)
    for layer in range(depth):
        li = layer // 2
        g_mix = norm_mix_g[layer][None, :]
        g_ffn = norm_ffn_g[layer][None, :]
        w1, w3, w2 = (w[layer].astype(BF16) for w in (ffn_w1, ffn_w3, ffn_w2))
        if layer % 2 == 0:
            wc, wt, qg, kg = _even_in_weights(ev_w_in[li], ev_q_norm_g[li], ev_k_norm_g[li])
            qpT, k, vaugT, qipT, kiw, wrow, u = _even_proj(xf, g_mix, wc, wt, qg, kg, tm)
            bound = (HEAD_DIM ** 0.5) * jnp.max(jnp.abs(ev_q_norm_g[li])) * jnp.max(jnp.abs(ev_k_norm_g[li]))
            bound = (ROUNDING_SLACK * bound).astype(F32).reshape(1)
            attn = _dsa_attention(bound, qpT, k, vaugT, qipT, kiw, wrow, batch, seq)
            pool = (u, ev_pool_w[li].astype(BF16), ev_pool_scale[li][None, :])
            xf = _post(xf, attn, ev_w_out[li].astype(BF16), g_ffn, w1, w3, w2, tm, seq, pool=pool)
        else:
            wk, wt = _odd_in_weights(od_w_qkv[li])
            qT, k, vT = _odd_proj(xf, g_mix, wk, wt, tm)
            attn = _sb_attention(qT, k, vT, batch, seq)
            xf = _post(xf, attn, od_w_out[li].astype(BF16), g_ffn, w1, w3, w2, tm, seq)
    return xf.reshape(batch, seq, d)


def kernel(x, norm_mix_g, norm_ffn_g, ev_w_in, ev_w_out, ev_q_norm_g, ev_k_norm_g, ev_pool_w,
           ev_pool_scale, od_w_qkv, od_w_out, ffn_w1, ffn_w3, ffn_w2):
    return _trunk(x, norm_mix_g, norm_ffn_g, ev_w_in, ev_w_out, ev_q_norm_g, ev_k_norm_g, ev_pool_w,
                  ev_pool_scale, od_w_qkv, od_w_out, ffn_w1, ffn_w3, ffn_w2, tm=512)
```
